```python
import math
import jax
import jax.numpy as jnp
from jax import lax
import numpy as np


D_MODEL = 1024
BATCH = 16
SEQ = 2048
DEPTH = 2

N_A_LAYERS = DEPTH // 2
N_B_LAYERS = DEPTH - N_A_LAYERS

SSM_GROUP = 16
SSM_GROUPS = D_MODEL // SSM_GROUP
SSM_STATE = 64
DT_MIN = 0.001
DT_MAX = 0.1

HEAD_DIM = 64
HEADS_PER_GROUP = D_MODEL // HEAD_DIM
DILATED_GROUPS = ((128, 1), (512, 4), (2048, 16))
N_DIL = len(DILATED_GROUPS)
BAND = DILATED_GROUPS[0][0] // DILATED_GROUPS[0][1]
MAX_DIL = max(d for _, d in DILATED_GROUPS)
ATT_WIDTH = N_DIL * HEADS_PER_GROUP * HEAD_DIM
MERGED_WIDTH = HEADS_PER_GROUP * HEAD_DIM
NEG_BIG = -1e30

REL_BUCKETS = 32
REL_MAX_DIST = 2048

D_FF = 2816
CONV_WIDTH = 3

DN_ALPHA = (2.0 * DEPTH) ** 0.25
DN_BETA = (8.0 * DEPTH) ** -0.25
LN_EPS = 1e-5

kernel_name = 'yoco_s5_dilated_attn_deepnorm_trunk'


def layer_norm(x, gain, bias):
    xf = x.astype(jnp.float32)
    mu = jnp.mean(xf, axis=-1, keepdims=True)
    var = jnp.mean(jnp.square(xf - mu), axis=-1, keepdims=True)
    y = (xf - mu) * lax.rsqrt(var + LN_EPS) * gain.astype(jnp.float32) + bias.astype(jnp.float32)
    return y.astype(x.dtype)


def post_norm(x, f, gain, bias):
    return layer_norm(DN_ALPHA * x + f.astype(x.dtype), gain, bias)


def _complex_affine_combine(e1, e2):
    a1r, a1i, b1r, b1i = e1
    a2r, a2i, b2r, b2i = e2
    return (a2r * a1r - a2i * a1i,
            a2r * a1i + a2i * a1r,
            a2r * b1r - a2i * b1i + b2r,
            a2r * b1i + a2i * b1r + b2i)


def s5_mixer(x, lam_re, lam_im, log_dt, b_re, b_im, c_re, c_im, d_skip, w_glu, b_glu, w_out):
    f32 = jnp.float32
    bsz, seq, _ = x.shape
    u = x.astype(f32).reshape(bsz, seq, SSM_GROUPS, SSM_GROUP)
    lr, li = lam_re.astype(f32), lam_im.astype(f32)
    dt = jnp.exp(log_dt.astype(f32))[:, None]
    mag = jnp.exp(lr * dt)
    ab_r, ab_i = mag * jnp.cos(li * dt), mag * jnp.sin(li * dt)
    den = lr * lr + li * li
    nr = ab_r - 1.0
    co_r = (nr * lr + ab_i * li) / den
    co_i = (ab_i * lr - nr * li) / den
    br, bi = b_re.astype(f32), b_im.astype(f32)
    bb_r = co_r[..., None] * br - co_i[..., None] * bi
    bb_i = co_r[..., None] * bi + co_i[..., None] * br
    bu_r = jnp.einsum('blgh,gph->blgp', u, bb_r)
    bu_i = jnp.einsum('blgh,gph->blgp', u, bb_i)
    a_r = jnp.broadcast_to(ab_r, (1, seq, SSM_GROUPS, SSM_STATE))
    a_i = jnp.broadcast_to(ab_i, (1, seq, SSM_GROUPS, SSM_STATE))
    _, _, h_r, h_i = lax.associative_scan(_complex_affine_combine, (a_r, a_i, bu_r, bu_i), axis=1)
    y = (jnp.einsum('blgp,ghp->blgh', h_r, c_re.astype(f32))
         - jnp.einsum('blgp,ghp->blgh', h_i, c_im.astype(f32))
         + d_skip.astype(f32) * u)
    y = jax.nn.gelu(y.reshape(bsz, seq, D_MODEL))
    g = y * jax.nn.sigmoid(y @ w_glu.astype(f32) + b_glu.astype(f32))
    return (g @ w_out.astype(f32)).astype(x.dtype)


def conv_glu_ffn(x, w_up, conv_w, conv_b, w_down):
    seq = x.shape[1]
    hcat = x @ w_up
    hp = jnp.pad(hcat, ((0, 0), (CONV_WIDTH - 1, 0), (0, 0)))
    hcat = conv_b + sum(conv_w[k] * hp[:, CONV_WIDTH - 1 - k:CONV_WIDTH - 1 - k + seq]
                        for k in range(CONV_WIDTH))
    val, gate = jnp.split(hcat, 2, axis=-1)
    return (jax.nn.silu(gate) * val) @ w_down


def _padded_len(seq):
    span = BAND * MAX_DIL
    return -(-seq // span) * span


def _to_residue_blocks(t, dil):
    bsz, lp, h, e = t.shape
    m = lp // dil
    t = t.reshape(bsz, m, dil, h, e).transpose(0, 2, 1, 3, 4)
    return t.reshape(bsz, dil, m // BAND, BAND, h, e)


def _from_residue_blocks(t, dil):
    bsz = t.shape[0]
    rest = t.shape[4:]
    t = t.reshape((bsz, dil, -1) + rest)
    t = jnp.moveaxis(t, 1, 2)
    return t.reshape((bsz, -1) + rest)


def _with_previous_block(t):
    prev = jnp.concatenate([jnp.zeros_like(t[:, :, :1]), t[:, :, :-1]], axis=2)
    return jnp.concatenate([prev, t], axis=3)


def _t5_bucket(dist):
    exact = REL_BUCKETS // 2
    d = np.maximum(dist, 1).astype(np.float32)
    large = exact + (np.log(d / exact) / math.log(REL_MAX_DIST / exact)
                     * (REL_BUCKETS - exact)).astype(np.int64)
    large = np.minimum(large, REL_BUCKETS - 1)
    return np.where(dist < exact, dist, large).astype(np.int32)


def _group_bias_mask(rel_bias, g, dil, n_blocks):
    steps = np.arange(BAND)[:, None] + BAND - np.arange(2 * BAND)[None, :]
    bucket = _t5_bucket(np.maximum(steps, 0) * dil)
    cols = rel_bias[:, g * HEADS_PER_GROUP:(g + 1) * HEADS_PER_GROUP]
    bias = jnp.transpose(cols[bucket], (2, 0, 1)).astype(jnp.float32)
    in_band = (steps >= 0) & (steps <= BAND)
    has_prev = (np.arange(n_blocks)[:, None, None] > 0) | (np.arange(2 * BAND)[None, None, :] >= BAND)
    valid = jnp.asarray(in_band[None] & has_prev)
    return bias, valid


def shared_kv(h, w_kv):
    bsz, seq, _ = h.shape
    lp = _padded_len(seq)
    kv = (h @ w_kv).astype(jnp.float32)
    kv = jnp.pad(kv, ((0, 0), (0, lp - seq), (0, 0)))
    kv = kv.reshape(bsz, lp, 2, N_DIL, HEADS_PER_GROUP, HEAD_DIM)
    blocks = []
    for g, (_, dil) in enumerate(DILATED_GROUPS):
        blocks.append(_with_previous_block(_to_residue_blocks(kv[:, :, 0, g], dil)))
        blocks.append(_with_previous_block(_to_residue_blocks(kv[:, :, 1, g], dil)))
    return blocks


def dilated_attention(h, w_q, w_out, rel_bias, kv_blocks):
    bsz, seq, _ = h.shape
    lp = _padded_len(seq)
    q = (h @ w_q).astype(jnp.float32) * (HEAD_DIM ** -0.5)
    q = jnp.pad(q, ((0, 0), (0, lp - seq), (0, 0))).reshape(bsz, lp, N_DIL, HEADS_PER_GROUP, HEAD_DIM)
    outs, lses = [], []
    for g, (_, dil) in enumerate(DILATED_GROUPS):
        kb, vb = kv_blocks[2 * g], kv_blocks[2 * g + 1]
        qb = _to_residue_blocks(q[:, :, g], dil)
        bias, valid = _group_bias_mask(rel_bias, g, dil, qb.shape[2])
        s = jnp.einsum('brnqhe,brnkhe->brnhqk', qb, kb) + bias
        s = jnp.where(valid[:, None], s, NEG_BIG)
        lse = jax.nn.logsumexp(s, axis=-1)
        p = jnp.exp(s - lse[..., None])
        o = jnp.einsum('brnhqk,brnkhe->brnqhe', p, vb)
        outs.append(_from_residue_blocks(o, dil)[:, :seq])
        lses.append(_from_residue_blocks(jnp.swapaxes(lse, -1, -2), dil)[:, :seq])
    wts = jax.nn.softmax(jnp.stack(lses), axis=0)
    o = jnp.einsum('gblh,gblhe->blhe', wts, jnp.stack(outs))
    return o.reshape(bsz, seq, MERGED_WIDTH).astype(h.dtype) @ w_out


def setup_inputs(seed: int = 0) -> dict:
    key = jax.random.key(seed)
    ks = jax.random.split(key, 24)
    f32 = jnp.float32
    na, nbl, g, p, gs = N_A_LAYERS, N_B_LAYERS, SSM_GROUPS, SSM_STATE, SSM_GROUP

    def nrm(k, shape, scale):
        return jax.random.normal(k, shape, f32) * scale

    x = nrm(ks[0], (BATCH, SEQ, D_MODEL), 1.0)
    s5_lam_re = -0.5 + nrm(ks[1], (na, g, p), 0.01)
    s5_lam_im = math.pi * jnp.arange(p, dtype=f32) + nrm(ks[2], (na, g, p), 0.01)
    s5_log_dt = jax.random.uniform(ks[3], (na, g), f32, math.log(DT_MIN), math.log(DT_MAX))
    s5_b_re = nrm(ks[4], (na, g, p, gs), (2.0 * gs) ** -0.5)
    s5_b_im = nrm(ks[5], (na, g, p, gs), (2.0 * gs) ** -0.5)
    s5_c_re = nrm(ks[6], (na, g, gs, p), p ** -0.5)
    s5_c_im = nrm(ks[7], (na, g, gs, p), p ** -0.5)
    s5_d = nrm(ks[8], (na, g, gs), 1.0)
    s5_w_glu = nrm(ks[9], (na, D_MODEL, D_MODEL), D_MODEL ** -0.5)
    s5_b_glu = nrm(ks[10], (na, D_MODEL), 0.01)
    s5_w_out = nrm(ks[11], (na, D_MODEL, D_MODEL), D_MODEL ** -0.5 * DN_BETA)
    w_k = nrm(ks[12], (D_MODEL, ATT_WIDTH), D_MODEL ** -0.5)
    w_v = nrm(ks[13], (D_MODEL, ATT_WIDTH), D_MODEL ** -0.5 * DN_BETA)
    attn_w_kv = jnp.concatenate([w_k, w_v], axis=1)
    attn_w_q = nrm(ks[14], (nbl, D_MODEL, ATT_WIDTH), D_MODEL ** -0.5)
    attn_w_out = nrm(ks[15], (nbl, MERGED_WIDTH, D_MODEL), MERGED_WIDTH ** -0.5 * DN_BETA)
    rel_bias = nrm(ks[16], (REL_BUCKETS, N_DIL * HEADS_PER_GROUP), 0.5)
    ffn_w_up = nrm(ks[17], (DEPTH, D_MODEL, 2 * D_FF), D_MODEL ** -0.5 * DN_BETA)
    ffn_conv_w = nrm(ks[18], (DEPTH, CONV_WIDTH, 2 * D_FF), CONV_WIDTH ** -0.5)
    ffn_conv_b = nrm(ks[19], (DEPTH, 2 * D_FF), 0.01)
    ffn_w_down = nrm(ks[20], (DEPTH, D_FF, D_MODEL), D_FF ** -0.5 * DN_BETA)
    ln_gain = 1.0 + nrm(ks[21], (DEPTH, 2, D_MODEL), 0.01)
    ln_bias = nrm(ks[22], (DEPTH, 2, D_MODEL), 0.01)
    return {'x': x, 's5_lam_re': s5_lam_re, 's5_lam_im': s5_lam_im, 's5_log_dt': s5_log_dt,
            's5_b_re': s5_b_re, 's5_b_im': s5_b_im, 's5_c_re': s5_c_re, 's5_c_im': s5_c_im,
            's5_d': s5_d, 's5_w_glu': s5_w_glu, 's5_b_glu': s5_b_glu, 's5_w_out': s5_w_out,
            'attn_w_kv': attn_w_kv, 'attn_w_q': attn_w_q, 'attn_w_out': attn_w_out,
            'rel_bias': rel_bias, 'ffn_w_up': ffn_w_up, 'ffn_conv_w': ffn_conv_w,
            'ffn_conv_b': ffn_conv_b, 'ffn_w_down': ffn_w_down, 'ln_gain': ln_gain, 'ln_bias': ln_bias}


def reference(x, s5_lam_re, s5_lam_im, s5_log_dt, s5_b_re, s5_b_im, s5_c_re, s5_c_im,
              s5_d, s5_w_glu, s5_b_glu, s5_w_out, attn_w_kv, attn_w_q, attn_w_out,
              rel_bias, ffn_w_up, ffn_conv_w, ffn_conv_b, ffn_w_down, ln_gain, ln_bias):
    h = x
    kv_blocks = None
    for layer in range(DEPTH):
        if layer < N_A_LAYERS:
            i = layer
            mix = s5_mixer(h, s5_lam_re[i], s5_lam_im[i], s5_log_dt[i], s5_b_re[i], s5_b_im[i],
                           s5_c_re[i], s5_c_im[i], s5_d[i], s5_w_glu[i], s5_b_glu[i], s5_w_out[i])
        else:
            j = layer - N_A_LAYERS
            mix = dilated_attention(h, attn_w_q[j], attn_w_out[j], rel_bias, kv_blocks)
        h = post_norm(h, mix, ln_gain[layer, 0], ln_bias[layer, 0])
        ffn = conv_glu_ffn(h, ffn_w_up[layer], ffn_conv_w[layer], ffn_conv_b[layer], ffn_w_down[layer])
        h = post_norm(h, ffn, ln_gain[layer, 1], ln_bias[layer, 1])
        if layer == N_A_LAYERS - 1:
            kv_blocks = shared_kv(h, attn_w_kv)
    return h
```

```python
import functools
import math

import numpy as np
import jax
import jax.numpy as jnp
from jax import lax
from jax.experimental import pallas as pl
from jax.experimental.pallas import tpu as pltpu

F32 = jnp.float32
BF16 = jnp.bfloat16

D_MODEL = 1024
BATCH = 16
SEQ = 2048
DEPTH = 2
SSM_GROUP = 16
SSM_GROUPS = D_MODEL // SSM_GROUP
SSM_STATE = 64
HEAD_DIM = 64
HEADS = D_MODEL // HEAD_DIM
DILATIONS = (1, 4, 16)
N_DIL = len(DILATIONS)
BAND = 128
NEG_BIG = -1e30
REL_BUCKETS = 32
REL_MAX_DIST = 2048
D_FF = 2816
CONV_WIDTH = 3
DN_ALPHA = (2.0 * DEPTH) ** 0.25
LN_EPS = 1e-5

LANES = 128
SUBLANES = 8
N_SLABS = D_MODEL // LANES
VMEM_LIMIT = 56 * 1024 * 1024

S5_LC = 32
S5_OCT = SUBLANES
S5_ROWS = S5_LC * S5_OCT
N_PAIRS = SSM_GROUPS // 2
PAIR_W = 2 * LANES
STATE_W = N_PAIRS * PAIR_W
FFN_LT = 512
FFN_FC = 256
PROJ_RC = 512
HEADS_PER_STEP = 4
ATT_W = HEADS_PER_STEP * HEAD_DIM
ATT_SLABS = ATT_W // LANES


def _layer_norm(v, gain, bias):
    mu = jnp.mean(v, axis=-1, keepdims=True)
    d = v - mu
    var = jnp.mean(d * d, axis=-1, keepdims=True)
    return d * lax.rsqrt(var + LN_EPS) * gain + bias


def _const_spec(shape):
    n = len(shape)
    return pl.BlockSpec(shape, lambda *_: (0,) * n, pipeline_mode=pl.Buffered(1))


def _params(*sem):
    return pltpu.CompilerParams(dimension_semantics=sem, vmem_limit_bytes=VMEM_LIMIT)


def _t5_bucket(dist):
    exact = REL_BUCKETS // 2
    d = np.maximum(dist, 1).astype(np.float32)
    large = exact + (np.log(d / exact) / math.log(REL_MAX_DIST / exact)
                     * (REL_BUCKETS - exact)).astype(np.int64)
    large = np.minimum(large, REL_BUCKETS - 1)
    return np.where(dist < exact, dist, large).astype(np.int32)


def _bucket_tables():
    steps = np.arange(BAND)[:, None] + BAND - np.arange(2 * BAND)[None, :]
    buckets = np.stack([_t5_bucket(np.maximum(steps, 0) * d) for d in DILATIONS])
    in_band = ((steps >= 0) & (steps <= BAND)).astype(np.int32)
    return buckets.astype(np.int32), in_band


def _bias_kernel(rb_ref, bk_ref, o_ref):
    g = pl.program_id(0)
    bk = bk_ref[0]
    for h in range(HEADS):
        acc = jnp.zeros((BAND, 2 * BAND), F32)
        for b in range(REL_BUCKETS):
            acc = jnp.where(bk == b, rb_ref[b, g * HEADS + h], acc)
        o_ref[0, h] = acc


def _bias_tiles(rel_bias, buckets):
    return pl.pallas_call(
        _bias_kernel,
        grid=(N_DIL,),
        in_specs=[pl.BlockSpec(memory_space=pltpu.SMEM),
                  pl.BlockSpec((1, BAND, 2 * BAND), lambda g: (g, 0, 0))],
        out_specs=pl.BlockSpec((1, HEADS, BAND, 2 * BAND), lambda g: (g, 0, 0, 0)),
        out_shape=jax.ShapeDtypeStruct((N_DIL, HEADS, BAND, 2 * BAND), F32),
        compiler_params=_params("arbitrary"),
        name="rel_bias_tiles",
    )(rel_bias, buckets)


def _s5_kernel(x_ref, a_ref, bb_ref, cc_ref, d_ref, wglu_ref, bglu_ref, wout_ref, g_ref, b_ref,
               o_ref, ut_ref, st_ref, h_ref, yt_ref, ystd_ref):
    @pl.when(pl.program_id(1) == 0)
    def _():
        h_ref[...] = jnp.zeros_like(h_ref)

    for b in range(S5_OCT):
        for k in range(N_SLABS):
            ut_ref[k, pl.ds(b, S5_LC, stride=S5_OCT), :] = x_ref[b, :, k * LANES:(k + 1) * LANES]

    for p in range(N_PAIRS):
        lhs = ut_ref[p // 4].astype(BF16)
        st_ref[:, p * PAIR_W:(p + 1) * PAIR_W] = jnp.dot(lhs, bb_ref[p], preferred_element_type=F32)

    pairs_per_sweep = 4
    for s in range(N_PAIRS // pairs_per_sweep):
        pairs = range(s * pairs_per_sweep, (s + 1) * pairs_per_sweep)
        ar = [a_ref[0, :, p * LANES:(p + 1) * LANES] for p in pairs]
        ai = [a_ref[1, :, p * LANES:(p + 1) * LANES] for p in pairs]
        hr0 = tuple(h_ref[:, p * PAIR_W:p * PAIR_W + LANES] for p in pairs)
        hi0 = tuple(h_ref[:, p * PAIR_W + LANES:(p + 1) * PAIR_W] for p in pairs)

        def step(t, carry, pairs=pairs, ar=ar, ai=ai):
            hr, hi = carry
            row = pl.multiple_of(t * S5_OCT, S5_OCT)
            nr, ni = [], []
            for q, p in enumerate(pairs):
                re_cols = slice(p * PAIR_W, p * PAIR_W + LANES)
                im_cols = slice(p * PAIR_W + LANES, (p + 1) * PAIR_W)
                new_r = ar[q] * hr[q] - ai[q] * hi[q] + st_ref[pl.ds(row, S5_OCT), re_cols]
                new_i = ar[q] * hi[q] + ai[q] * hr[q] + st_ref[pl.ds(row, S5_OCT), im_cols]
                st_ref[pl.ds(row, S5_OCT), re_cols] = new_r
                st_ref[pl.ds(row, S5_OCT), im_cols] = new_i
                nr.append(new_r)
                ni.append(new_i)
            return tuple(nr), tuple(ni)

        hr, hi = lax.fori_loop(0, S5_LC, step, (hr0, hi0), unroll=4)
        for q, p in enumerate(pairs):
            h_ref[:, p * PAIR_W:p * PAIR_W + LANES] = hr[q]
            h_ref[:, p * PAIR_W + LANES:(p + 1) * PAIR_W] = hi[q]

    for k in range(N_SLABS):
        lhs = st_ref[:, k * D_MODEL:(k + 1) * D_MODEL].astype(BF16)
        yt_ref[k] = jnp.dot(lhs, cc_ref[k], preferred_element_type=F32)

    for b in range(S5_OCT):
        yb = jnp.concatenate([yt_ref[k, pl.ds(b, S5_LC, stride=S5_OCT), :] for k in range(N_SLABS)], axis=-1)
        ystd_ref[b * S5_LC:(b + 1) * S5_LC, :] = jax.nn.gelu(yb + d_ref[...] * x_ref[b])

    y = ystd_ref[...]
    z = jnp.dot(y.astype(BF16), wglu_ref[...], preferred_element_type=F32) + bglu_ref[...]
    gated = y * jax.nn.sigmoid(z)
    mix = jnp.dot(gated.astype(BF16), wout_ref[...], preferred_element_type=F32)
    x_all = x_ref[...].reshape(S5_ROWS, D_MODEL)
    out = _layer_norm(DN_ALPHA * x_all + mix, g_ref[...], b_ref[...])
    o_ref[...] = out.reshape(S5_OCT, S5_LC, D_MODEL)


def _s5_discretise(lam_re, lam_im, log_dt, b_re, b_im, c_re, c_im):
    lr, li = lam_re.astype(F32), lam_im.astype(F32)
    dt = jnp.exp(log_dt.astype(F32))[:, None]
    mag = jnp.exp(lr * dt)
    ab_r, ab_i = mag * jnp.cos(li * dt), mag * jnp.sin(li * dt)
    den = lr * lr + li * li
    nr = ab_r - 1.0
    co_r = (nr * lr + ab_i * li) / den
    co_i = (ab_i * lr - nr * li) / den
    br, bi = b_re.astype(F32), b_im.astype(F32)
    bb_r = co_r[..., None] * br - co_i[..., None] * bi
    bb_i = co_r[..., None] * bi + co_i[..., None] * br

    eye2 = jnp.eye(2, dtype=F32)

    a = jnp.stack([ab_r.reshape(-1), ab_i.reshape(-1)])
    a = jnp.broadcast_to(a[:, None, :], (2, SUBLANES, SSM_GROUPS * SSM_STATE))

    def pair_blockdiag(m):
        g, r, c = m.shape
        m = m.reshape(g // 2, 2, r, c)
        out = jnp.einsum('pjrc,jk->pjrkc', m, eye2)
        return out.reshape(g // 2, 2 * r, 2 * c)

    wb = jnp.concatenate([pair_blockdiag(jnp.swapaxes(bb_r, 1, 2)),
                          pair_blockdiag(jnp.swapaxes(bb_i, 1, 2))], axis=-1)
    sel = jnp.asarray(np.eye(4, dtype=np.float32)[np.arange(N_PAIRS) % 4])
    bb = jnp.einsum('prn,pq->pqrn', wb, sel).reshape(N_PAIRS, LANES, PAIR_W)

    wc = jnp.concatenate([pair_blockdiag(jnp.swapaxes(c_re.astype(F32), 1, 2)),
                          -pair_blockdiag(jnp.swapaxes(c_im.astype(F32), 1, 2))], axis=1)
    wc = wc.reshape(N_SLABS, 4, PAIR_W, 2 * SSM_GROUP)
    eye4 = jnp.eye(4, dtype=F32)
    cc = jnp.einsum('kqrc,qs->kqrsc', wc, eye4).reshape(N_SLABS, 4 * PAIR_W, LANES)
    return a, bb.astype(BF16), cc.astype(BF16)


def _s5_layer(x, a, bb, cc, d_skip, w_glu, b_glu, w_out, gain, bias):
    n_oct = BATCH // S5_OCT
    return pl.pallas_call(
        _s5_kernel,
        grid=(n_oct, SEQ // S5_LC),
        in_specs=[
            pl.BlockSpec((S5_OCT, S5_LC, D_MODEL), lambda o, c: (o, c, 0)),
            _const_spec(a.shape), _const_spec(bb.shape), _const_spec(cc.shape),
            _const_spec(d_skip.shape), _const_spec(w_glu.shape), _const_spec(b_glu.shape),
            _const_spec(w_out.shape), _const_spec(gain.shape), _const_spec(bias.shape),
        ],
        out_specs=pl.BlockSpec((S5_OCT, S5_LC, D_MODEL), lambda o, c: (o, c, 0)),
        out_shape=jax.ShapeDtypeStruct((BATCH, SEQ, D_MODEL), F32),
        scratch_shapes=[
            pltpu.VMEM((N_SLABS, S5_ROWS, LANES), F32),
            pltpu.VMEM((S5_ROWS, STATE_W), F32),
            pltpu.VMEM((S5_OCT, STATE_W), F32),
            pltpu.VMEM((N_SLABS, S5_ROWS, LANES), F32),
            pltpu.VMEM((S5_ROWS, D_MODEL), F32),
        ],
        compiler_params=_params("arbitrary", "arbitrary"),
        name="s5_glu_postnorm",
    )(x, a, bb, cc, d_skip, w_glu, b_glu, w_out, gain, bias)


def _ffn_kernel(x_ref, wup_ref, cw_ref, cb_ref, wdown_ref, g_ref, b_ref, o_ref,
                carry_ref, pad_ref, act_ref, *, slab_out):
    @pl.when(pl.program_id(1) == 0)
    def _():
        carry_ref[...] = jnp.zeros_like(carry_ref)

    x = x_ref[0]
    xb = x.astype(BF16)
    halo = SUBLANES
    for c in range(D_FF // FFN_FC):
        conv = []
        for part in range(2):
            cols = slice(part * D_FF + c * FFN_FC, part * D_FF + (c + 1) * FFN_FC)
            up = jnp.dot(xb, wup_ref[:, cols], preferred_element_type=F32)
            pad_ref[halo - 2:halo, :] = carry_ref[:, cols]
            pad_ref[halo:halo + FFN_LT, :] = up
            carry_ref[:, cols] = up[FFN_LT - 2:, :]
            back1 = pad_ref[halo - 1:halo - 1 + FFN_LT, :]
            back2 = pad_ref[halo - 2:halo - 2 + FFN_LT, :]
            conv.append(cb_ref[:, cols] + cw_ref[0:1, cols] * up + cw_ref[1:2, cols] * back1
                        + cw_ref[2:3, cols] * back2)
        val, gate = conv
        act_ref[:, c * FFN_FC:(c + 1) * FFN_FC] = (jax.nn.silu(gate) * val).astype(BF16)

    ffn = jnp.dot(act_ref[...], wdown_ref[...], preferred_element_type=F32)
    out = _layer_norm(DN_ALPHA * x + ffn, g_ref[...], b_ref[...])
    if slab_out:
        for k in range(N_SLABS):
            o_ref[0, k] = out[:, k * LANES:(k + 1) * LANES]
    else:
        o_ref[0] = out


def _ffn_layer(x, w_up, conv_w, conv_b, w_down, gain, bias, *, slab_out):
    if slab_out:
        out_spec = pl.BlockSpec((1, N_SLABS, FFN_LT, LANES), lambda b, j: (b, 0, j, 0))
        out_shape = jax.ShapeDtypeStruct((BATCH, N_SLABS, SEQ, LANES), F32)
    else:
        out_spec = pl.BlockSpec((1, FFN_LT, D_MODEL), lambda b, j: (b, j, 0))
        out_shape = jax.ShapeDtypeStruct((BATCH, SEQ, D_MODEL), F32)
    return pl.pallas_call(
        functools.partial(_ffn_kernel, slab_out=slab_out),
        grid=(BATCH, SEQ // FFN_LT),
        in_specs=[
            pl.BlockSpec((1, FFN_LT, D_MODEL), lambda b, j: (b, j, 0)),
            _const_spec(w_up.shape), _const_spec(conv_w.shape), _const_spec(conv_b.shape),
            _const_spec(w_down.shape), _const_spec(gain.shape), _const_spec(bias.shape),
        ],
        out_specs=out_spec,
        out_shape=out_shape,
        scratch_shapes=[
            pltpu.VMEM((CONV_WIDTH - 1, 2 * D_FF), F32),
            pltpu.VMEM((SUBLANES + FFN_LT, FFN_FC), F32),
            pltpu.VMEM((FFN_LT, D_FF), BF16),
        ],
        compiler_params=_params("arbitrary", "arbitrary"),
        name="conv_glu_ffn_postnorm_slab" if slab_out else "conv_glu_ffn_postnorm",
    )(x, w_up, conv_w, conv_b, w_down, gain, bias)


def _qkv_kernel(h_ref, w_ref, o_ref, lhs_ref):
    g = pl.program_id(1)
    which = pl.program_id(2)

    for gi, dil in enumerate(DILATIONS):
        @pl.when(jnp.logical_and(g == gi, which == 0))
        def _(dil=dil):
            slots = SEQ // dil
            for r in range(dil):
                for k in range(N_SLABS):
                    if dil == 1:
                        rows = h_ref[0, k]
                    else:
                        rows = h_ref[0, k, pl.ds(r, slots, stride=dil), :]
                    lhs_ref[r * slots:(r + 1) * slots, k * LANES:(k + 1) * LANES] = rows.astype(BF16)

    scale = jnp.where(which == 0, HEAD_DIM ** -0.5, 1.0).astype(F32)
    for c in range(SEQ // PROJ_RC):
        rows = slice(c * PROJ_RC, (c + 1) * PROJ_RC)
        res = jnp.dot(lhs_ref[rows, :], w_ref[0, 0], preferred_element_type=F32)
        o_ref[0, 0, rows, :] = (res * scale).astype(BF16)


def _qkv_project(h_slabs, w_qkv):
    return pl.pallas_call(
        _qkv_kernel,
        grid=(BATCH, N_DIL, 3),
        in_specs=[
            pl.BlockSpec((1, N_SLABS, SEQ, LANES), lambda b, g, w: (b, 0, 0, 0)),
            pl.BlockSpec((1, 1, D_MODEL, D_MODEL), lambda b, g, w: (g, w, 0, 0)),
        ],
        out_specs=pl.BlockSpec((1, 1, SEQ, D_MODEL), lambda b, g, w: (b, g * 3 + w, 0, 0)),
        out_shape=jax.ShapeDtypeStruct((BATCH, 3 * N_DIL, SEQ, D_MODEL), BF16),
        scratch_shapes=[pltpu.VMEM((SEQ, D_MODEL), BF16)],
        compiler_params=_params("arbitrary", "arbitrary", "arbitrary"),
        name="qkv_project_regroup",
    )(h_slabs, w_qkv)


def _attn_kernel(qkv_ref, bias_ref, mask_ref, o_ref, og_ref, lse_ref):
    valid_full = mask_ref[...] != 0
    valid_cur = mask_ref[:, BAND:] != 0
    lane = lax.broadcasted_iota(jnp.int32, (BAND, LANES), 1)
    head_lanes = [lane < HEAD_DIM, lane >= HEAD_DIM]

    def block_attention(g, row0, key0, n_keys, slab):
        cols = slice(slab * LANES, (slab + 1) * LANES)
        q2 = qkv_ref[0, 3 * g, pl.ds(row0, BAND), cols]
        k2 = qkv_ref[0, 3 * g + 1, pl.ds(key0, n_keys), cols]
        v2 = qkv_ref[0, 3 * g + 2, pl.ds(key0, n_keys), cols]
        valid = valid_full if n_keys == 2 * BAND else valid_cur
        o_pair, lse_pair = None, None
        for hh in range(2):
            qm = jnp.where(head_lanes[hh], q2, jnp.zeros_like(q2))
            s = lax.dot_general(qm, k2, (((1,), (1,)), ((), ())), preferred_element_type=F32)
            if n_keys == BAND:
                bias = bias_ref[g, 2 * slab + hh, :, BAND:]
            else:
                bias = bias_ref[g, 2 * slab + hh]
            s = jnp.where(valid, s + bias, NEG_BIG)
            m = jnp.max(s, axis=-1, keepdims=True)
            p = jnp.exp(s - m)
            l = jnp.sum(p, axis=-1, keepdims=True)
            pv = jnp.dot(p.astype(BF16), v2, preferred_element_type=F32)
            o_h = pv / l
            lse_h = jnp.broadcast_to(m + jnp.log(l), (BAND, LANES))
            if hh == 0:
                o_pair, lse_pair = o_h, lse_h
            else:
                o_pair = jnp.where(head_lanes[0], o_pair, o_h)
                lse_pair = jnp.where(head_lanes[0], lse_pair, lse_h)
        return o_pair, lse_pair

    for g in (1, 2):
        dil = DILATIONS[g]
        n_blocks = SEQ // dil // BAND

        def residue(r, _, g=g, dil=dil, n_blocks=n_blocks):
            def one_block(n, first):
                row0 = pl.multiple_of((r * n_blocks + n) * BAND, BAND)
                for slab in range(ATT_SLABS):
                    if first:
                        o, lse = block_attention(g, row0, row0, BAND, slab)
                    else:
                        key0 = pl.multiple_of(row0 - BAND, BAND)
                        o, lse = block_attention(g, row0, key0, 2 * BAND, slab)
                    dst = pl.ds(r + dil * BAND * n, BAND, stride=dil)
                    og_ref[g - 1, slab, dst, :] = o
                    lse_ref[g - 1, slab, dst, :] = lse

            one_block(0, True)
            if n_blocks > 1:
                def later(n, _):
                    one_block(n, False)
                    return 0
                lax.fori_loop(1, n_blocks, later, 0)
            return 0

        lax.fori_loop(0, dil, residue, 0)

    def merge_block(i, first):
        row0 = pl.multiple_of(i * BAND, BAND)
        for slab in range(ATT_SLABS):
            if first:
                o0, l0 = block_attention(0, row0, row0, BAND, slab)
            else:
                key0 = pl.multiple_of(row0 - BAND, BAND)
                o0, l0 = block_attention(0, row0, key0, 2 * BAND, slab)
            o1, l1 = og_ref[0, slab, pl.ds(row0, BAND), :], lse_ref[0, slab, pl.ds(row0, BAND), :]
            o2, l2 = og_ref[1, slab, pl.ds(row0, BAND), :], lse_ref[1, slab, pl.ds(row0, BAND), :]
            top = jnp.maximum(jnp.maximum(l0, l1), l2)
            w0, w1, w2 = jnp.exp(l0 - top), jnp.exp(l1 - top), jnp.exp(l2 - top)
            merged = (w0 * o0 + w1 * o1 + w2 * o2) / (w0 + w1 + w2)
            o_ref[0, pl.ds(row0, BAND), slab * LANES:(slab + 1) * LANES] = merged.astype(o_ref.dtype)

    merge_block(0, True)

    def later_blocks(i, _):
        merge_block(i, False)
        return 0
    lax.fori_loop(1, SEQ // BAND, later_blocks, 0)


def _attention(qkv, bias, mask):
    return pl.pallas_call(
        _attn_kernel,
        grid=(BATCH, HEADS // HEADS_PER_STEP),
        in_specs=[
            pl.BlockSpec((1, 3 * N_DIL, SEQ, ATT_W), lambda b, c: (b, 0, 0, c)),
            pl.BlockSpec((N_DIL, HEADS_PER_STEP, BAND, 2 * BAND), lambda b, c: (0, c, 0, 0)),
            _const_spec(mask.shape),
        ],
        out_specs=pl.BlockSpec((1, SEQ, ATT_W), lambda b, c: (b, 0, c)),
        out_shape=jax.ShapeDtypeStruct((BATCH, SEQ, D_MODEL), BF16),
        scratch_shapes=[
            pltpu.VMEM((N_DIL - 1, ATT_SLABS, SEQ, LANES), F32),
            pltpu.VMEM((N_DIL - 1, ATT_SLABS, SEQ, LANES), F32),
        ],
        compiler_params=_params("arbitrary", "arbitrary"),
        name="dilated_attention_merge",
    )(qkv, bias, mask)


def _outproj_kernel(a_ref, h_ref, w_ref, g_ref, b_ref, o_ref):
    res = jnp.concatenate([h_ref[0, k] for k in range(N_SLABS)], axis=-1)
    mix = jnp.dot(a_ref[0], w_ref[...], preferred_element_type=F32)
    o_ref[0] = _layer_norm(DN_ALPHA * res + mix, g_ref[...], b_ref[...])


def _outproj_layer(attn, h_slabs, w_out, gain, bias):
    return pl.pallas_call(
        _outproj_kernel,
        grid=(BATCH, SEQ // FFN_LT),
        in_specs=[
            pl.BlockSpec((1, FFN_LT, D_MODEL), lambda b, j: (b, j, 0)),
            pl.BlockSpec((1, N_SLABS, FFN_LT, LANES), lambda b, j: (b, 0, j, 0)),
            _const_spec(w_out.shape), _const_spec(gain.shape), _const_spec(bias.shape),
        ],
        out_specs=pl.BlockSpec((1, FFN_LT, D_MODEL), lambda b, j: (b, j, 0)),
        out_shape=jax.ShapeDtypeStruct((BATCH, SEQ, D_MODEL), F32),
        compiler_params=_params("arbitrary", "arbitrary"),
        name="attn_outproj_postnorm",
    )(attn, h_slabs, w_out, gain, bias)


def kernel(x, s5_lam_re, s5_lam_im, s5_log_dt, s5_b_re, s5_b_im, s5_c_re, s5_c_im, s5_d, s5_w_glu,
           s5_b_glu, s5_w_out, attn_w_kv, attn_w_q, attn_w_out, rel_bias, ffn_w_up, ffn_conv_w,
           ffn_conv_b, ffn_w_down, ln_gain, ln_bias):
    row = lambda v: v.astype(F32).reshape(1, -1)

    a, bb, cc = _s5_discretise(s5_lam_re[0], s5_lam_im[0], s5_log_dt[0], s5_b_re[0], s5_b_im[0],
                               s5_c_re[0], s5_c_im[0])
    h = _s5_layer(x, a, bb, cc, row(s5_d[0]), s5_w_glu[0].astype(BF16), row(s5_b_glu[0]),
                  s5_w_out[0].astype(BF16), row(ln_gain[0, 0]), row(ln_bias[0, 0]))
    h_slabs = _ffn_layer(h, ffn_w_up[0].astype(BF16), ffn_conv_w[0], row(ffn_conv_b[0]),
                         ffn_w_down[0].astype(BF16), row(ln_gain[0, 1]), row(ln_bias[0, 1]),
                         slab_out=True)

    split = lambda w: w.reshape(D_MODEL, N_DIL, D_MODEL)
    w_k, w_v = attn_w_kv[:, :N_DIL * D_MODEL], attn_w_kv[:, N_DIL * D_MODEL:]
    w_qkv = jnp.stack([split(attn_w_q[0]), split(w_k), split(w_v)], axis=0)
    w_qkv = jnp.transpose(w_qkv, (2, 0, 1, 3)).astype(BF16)
    qkv = _qkv_project(h_slabs, w_qkv)

    buckets, in_band = _bucket_tables()
    bias = _bias_tiles(rel_bias.astype(F32), jnp.asarray(buckets))
    attn = _attention(qkv, bias, jnp.asarray(in_band))
    h = _outproj_layer(attn, h_slabs, attn_w_out[0].astype(BF16), row(ln_gain[1, 0]), row(ln_bias[1, 0]))
    return _ffn_layer(h, ffn_w_up[1].astype(BF16), ffn_conv_w[1], row(ffn_conv_b[1]),
                      ffn_w_down[1].astype(BF16), row(ln_gain[1, 1]), row(ln_bias[1, 1]),
                      slab_out=False)
```

```python
import functools
import math

import numpy as np
import jax
import jax.numpy as jnp
from jax import lax
from jax.experimental import pallas as pl
from jax.experimental.pallas import tpu as pltpu

F32 = jnp.float32
BF16 = jnp.bfloat16

D_MODEL = 1024
BATCH = 16
SEQ = 2048
DEPTH = 2
SSM_GROUP = 16
SSM_GROUPS = D_MODEL // SSM_GROUP
SSM_STATE = 64
HEAD_DIM = 64
HEADS = D_MODEL // HEAD_DIM
DILATIONS = (1, 4, 16)
N_DIL = len(DILATIONS)
BAND = 128
NEG_BIG = -1e30
REL_BUCKETS = 32
REL_MAX_DIST = 2048
D_FF = 2816
CONV_WIDTH = 3
DN_ALPHA = (2.0 * DEPTH) ** 0.25
LN_EPS = 1e-5

LANES = 128
SUBLANES = 8
N_SLABS = D_MODEL // LANES
VMEM_LIMIT = 56 * 1024 * 1024

S5_LC = 32
S5_OCT = SUBLANES
S5_ROWS = S5_LC * S5_OCT
N_PAIRS = SSM_GROUPS // 2
PAIR_W = 2 * LANES
STATE_W = N_PAIRS * PAIR_W
FFN_LT = 512
FFN_FC = 256
PROJ_RC = 512
HEADS_PER_STEP = 4
ATT_W = HEADS_PER_STEP * HEAD_DIM
ATT_SLABS = ATT_W // LANES
ATT_LAG = 2
ATT_STEPS_PER_BODY = 2
ATT_RING = 2 * ATT_LAG


def _layer_norm(v, gain, bias):
    mu = jnp.mean(v, axis=-1, keepdims=True)
    d = v - mu
    var = jnp.mean(d * d, axis=-1, keepdims=True)
    return d * lax.rsqrt(var + LN_EPS) * gain + bias


def _const_spec(shape):
    n = len(shape)
    return pl.BlockSpec(shape, lambda *_: (0,) * n, pipeline_mode=pl.Buffered(1))


def _params(*sem):
    return pltpu.CompilerParams(dimension_semantics=sem, vmem_limit_bytes=VMEM_LIMIT)


def _t5_bucket(dist):
    exact = REL_BUCKETS // 2
    d = np.maximum(dist, 1).astype(np.float32)
    large = exact + (np.log(d / exact) / math.log(REL_MAX_DIST / exact)
                     * (REL_BUCKETS - exact)).astype(np.int64)
    large = np.minimum(large, REL_BUCKETS - 1)
    return np.where(dist < exact, dist, large).astype(np.int32)


def _bucket_tables():
    steps = np.arange(BAND)[:, None] + BAND - np.arange(2 * BAND)[None, :]
    buckets = np.stack([_t5_bucket(np.maximum(steps, 0) * d) for d in DILATIONS])
    in_band = ((steps >= 0) & (steps <= BAND)).astype(np.int32)
    first = in_band * (np.arange(2 * BAND)[None, :] >= BAND)
    return buckets.astype(np.int32), np.stack([in_band, first]).astype(np.int32)


def _bias_kernel(rb_ref, bk_ref, o_ref):
    g = pl.program_id(0)
    bk = bk_ref[0]
    for h in range(HEADS):
        acc = jnp.zeros((BAND, 2 * BAND), F32)
        for b in range(REL_BUCKETS):
            acc = jnp.where(bk == b, rb_ref[b, g * HEADS + h], acc)
        o_ref[0, h] = acc


def _bias_tiles(rel_bias, buckets):
    return pl.pallas_call(
        _bias_kernel,
        grid=(N_DIL,),
        in_specs=[pl.BlockSpec(memory_space=pltpu.SMEM),
                  pl.BlockSpec((1, BAND, 2 * BAND), lambda g: (g, 0, 0))],
        out_specs=pl.BlockSpec((1, HEADS, BAND, 2 * BAND), lambda g: (g, 0, 0, 0)),
        out_shape=jax.ShapeDtypeStruct((N_DIL, HEADS, BAND, 2 * BAND), F32),
        compiler_params=_params("arbitrary"),
        name="rel_bias_tiles",
    )(rel_bias, buckets)


def _s5_kernel(x_ref, a_ref, bb_ref, cc_ref, d_ref, wglu_ref, bglu_ref, wout_ref, g_ref, b_ref,
               o_ref, ut_ref, st_ref, h_ref, yt_ref, ystd_ref):
    @pl.when(pl.program_id(1) == 0)
    def _():
        h_ref[...] = jnp.zeros_like(h_ref)

    for b in range(S5_OCT):
        for k in range(N_SLABS):
            ut_ref[k, pl.ds(b, S5_LC, stride=S5_OCT), :] = x_ref[b, :, k * LANES:(k + 1) * LANES]

    for p in range(N_PAIRS):
        lhs = ut_ref[p // 4].astype(BF16)
        st_ref[:, p * PAIR_W:(p + 1) * PAIR_W] = jnp.dot(lhs, bb_ref[p], preferred_element_type=F32)

    pairs_per_sweep = 4
    for s in range(N_PAIRS // pairs_per_sweep):
        pairs = range(s * pairs_per_sweep, (s + 1) * pairs_per_sweep)
        ar = [a_ref[0, :, p * LANES:(p + 1) * LANES] for p in pairs]
        ai = [a_ref[1, :, p * LANES:(p + 1) * LANES] for p in pairs]
        hr0 = tuple(h_ref[:, p * PAIR_W:p * PAIR_W + LANES] for p in pairs)
        hi0 = tuple(h_ref[:, p * PAIR_W + LANES:(p + 1) * PAIR_W] for p in pairs)

        def step(t, carry, pairs=pairs, ar=ar, ai=ai):
            hr, hi = carry
            row = pl.multiple_of(t * S5_OCT, S5_OCT)
            nr, ni = [], []
            for q, p in enumerate(pairs):
                re_cols = slice(p * PAIR_W, p * PAIR_W + LANES)
                im_cols = slice(p * PAIR_W + LANES, (p + 1) * PAIR_W)
                new_r = ar[q] * hr[q] - ai[q] * hi[q] + st_ref[pl.ds(row, S5_OCT), re_cols]
                new_i = ar[q] * hi[q] + ai[q] * hr[q] + st_ref[pl.ds(row, S5_OCT), im_cols]
                st_ref[pl.ds(row, S5_OCT), re_cols] = new_r
                st_ref[pl.ds(row, S5_OCT), im_cols] = new_i
                nr.append(new_r)
                ni.append(new_i)
            return tuple(nr), tuple(ni)

        hr, hi = lax.fori_loop(0, S5_LC, step, (hr0, hi0), unroll=4)
        for q, p in enumerate(pairs):
            h_ref[:, p * PAIR_W:p * PAIR_W + LANES] = hr[q]
            h_ref[:, p * PAIR_W + LANES:(p + 1) * PAIR_W] = hi[q]

    for k in range(N_SLABS):
        lhs = st_ref[:, k * D_MODEL:(k + 1) * D_MODEL].astype(BF16)
        yt_ref[k] = jnp.dot(lhs, cc_ref[k], preferred_element_type=F32)

    for b in range(S5_OCT):
        yb = jnp.concatenate([yt_ref[k, pl.ds(b, S5_LC, stride=S5_OCT), :] for k in range(N_SLABS)], axis=-1)
        ystd_ref[b * S5_LC:(b + 1) * S5_LC, :] = jax.nn.gelu(yb + d_ref[...] * x_ref[b])

    y = ystd_ref[...]
    z = jnp.dot(y.astype(BF16), wglu_ref[...], preferred_element_type=F32) + bglu_ref[...]
    gated = y * jax.nn.sigmoid(z)
    mix = jnp.dot(gated.astype(BF16), wout_ref[...], preferred_element_type=F32)
    x_all = x_ref[...].reshape(S5_ROWS, D_MODEL)
    out = _layer_norm(DN_ALPHA * x_all + mix, g_ref[...], b_ref[...])
    o_ref[...] = out.reshape(S5_OCT, S5_LC, D_MODEL)


def _s5_discretise(lam_re, lam_im, log_dt, b_re, b_im, c_re, c_im):
    lr, li = lam_re.astype(F32), lam_im.astype(F32)
    dt = jnp.exp(log_dt.astype(F32))[:, None]
    mag = jnp.exp(lr * dt)
    ab_r, ab_i = mag * jnp.cos(li * dt), mag * jnp.sin(li * dt)
    den = lr * lr + li * li
    nr = ab_r - 1.0
    co_r = (nr * lr + ab_i * li) / den
    co_i = (ab_i * lr - nr * li) / den
    br, bi = b_re.astype(F32), b_im.astype(F32)
    bb_r = co_r[..., None] * br - co_i[..., None] * bi
    bb_i = co_r[..., None] * bi + co_i[..., None] * br

    eye2 = jnp.eye(2, dtype=F32)

    a = jnp.stack([ab_r.reshape(-1), ab_i.reshape(-1)])
    a = jnp.broadcast_to(a[:, None, :], (2, SUBLANES, SSM_GROUPS * SSM_STATE))

    def pair_blockdiag(m):
        g, r, c = m.shape
        m = m.reshape(g // 2, 2, r, c)
        out = jnp.einsum('pjrc,jk->pjrkc', m, eye2)
        return out.reshape(g // 2, 2 * r, 2 * c)

    wb = jnp.concatenate([pair_blockdiag(jnp.swapaxes(bb_r, 1, 2)),
                          pair_blockdiag(jnp.swapaxes(bb_i, 1, 2))], axis=-1)
    sel = jnp.asarray(np.eye(4, dtype=np.float32)[np.arange(N_PAIRS) % 4])
    bb = jnp.einsum('prn,pq->pqrn', wb, sel).reshape(N_PAIRS, LANES, PAIR_W)

    wc = jnp.concatenate([pair_blockdiag(jnp.swapaxes(c_re.astype(F32), 1, 2)),
                          -pair_blockdiag(jnp.swapaxes(c_im.astype(F32), 1, 2))], axis=1)
    wc = wc.reshape(N_SLABS, 4, PAIR_W, 2 * SSM_GROUP)
    eye4 = jnp.eye(4, dtype=F32)
    cc = jnp.einsum('kqrc,qs->kqrsc', wc, eye4).reshape(N_SLABS, 4 * PAIR_W, LANES)
    return a, bb.astype(BF16), cc.astype(BF16)


def _s5_layer(x, a, bb, cc, d_skip, w_glu, b_glu, w_out, gain, bias):
    n_oct = BATCH // S5_OCT
    return pl.pallas_call(
        _s5_kernel,
        grid=(n_oct, SEQ // S5_LC),
        in_specs=[
            pl.BlockSpec((S5_OCT, S5_LC, D_MODEL), lambda o, c: (o, c, 0)),
            _const_spec(a.shape), _const_spec(bb.shape), _const_spec(cc.shape),
            _const_spec(d_skip.shape), _const_spec(w_glu.shape), _const_spec(b_glu.shape),
            _const_spec(w_out.shape), _const_spec(gain.shape), _const_spec(bias.shape),
        ],
        out_specs=pl.BlockSpec((S5_OCT, S5_LC, D_MODEL), lambda o, c: (o, c, 0)),
        out_shape=jax.ShapeDtypeStruct((BATCH, SEQ, D_MODEL), F32),
        scratch_shapes=[
            pltpu.VMEM((N_SLABS, S5_ROWS, LANES), F32),
            pltpu.VMEM((S5_ROWS, STATE_W), F32),
            pltpu.VMEM((S5_OCT, STATE_W), F32),
            pltpu.VMEM((N_SLABS, S5_ROWS, LANES), F32),
            pltpu.VMEM((S5_ROWS, D_MODEL), F32),
        ],
        compiler_params=_params("arbitrary", "arbitrary"),
        name="s5_glu_postnorm",
    )(x, a, bb, cc, d_skip, w_glu, b_glu, w_out, gain, bias)


def _ffn_kernel(x_ref, wup_ref, cw_ref, cb_ref, wdown_ref, g_ref, b_ref, o_ref,
                carry_ref, pad_ref, act_ref, *, slab_out):
    @pl.when(pl.program_id(1) == 0)
    def _():
        carry_ref[...] = jnp.zeros_like(carry_ref)

    x = x_ref[0]
    xb = x.astype(BF16)
    halo = SUBLANES
    for c in range(D_FF // FFN_FC):
        conv = []
        for part in range(2):
            cols = slice(part * D_FF + c * FFN_FC, part * D_FF + (c + 1) * FFN_FC)
            up = jnp.dot(xb, wup_ref[:, cols], preferred_element_type=F32)
            pad_ref[halo - 2:halo, :] = carry_ref[:, cols]
            pad_ref[halo:halo + FFN_LT, :] = up
            carry_ref[:, cols] = up[FFN_LT - 2:, :]
            back1 = pad_ref[halo - 1:halo - 1 + FFN_LT, :]
            back2 = pad_ref[halo - 2:halo - 2 + FFN_LT, :]
            conv.append(cb_ref[:, cols] + cw_ref[0:1, cols] * up + cw_ref[1:2, cols] * back1
                        + cw_ref[2:3, cols] * back2)
        val, gate = conv
        act_ref[:, c * FFN_FC:(c + 1) * FFN_FC] = (jax.nn.silu(gate) * val).astype(BF16)

    ffn = jnp.dot(act_ref[...], wdown_ref[...], preferred_element_type=F32)
    out = _layer_norm(DN_ALPHA * x + ffn, g_ref[...], b_ref[...])
    if slab_out:
        for k in range(N_SLABS):
            o_ref[0, k] = out[:, k * LANES:(k + 1) * LANES]
    else:
        o_ref[0] = out


def _ffn_layer(x, w_up, conv_w, conv_b, w_down, gain, bias, *, slab_out):
    if slab_out:
        out_spec = pl.BlockSpec((1, N_SLABS, FFN_LT, LANES), lambda b, j: (b, 0, j, 0))
        out_shape = jax.ShapeDtypeStruct((BATCH, N_SLABS, SEQ, LANES), F32)
    else:
        out_spec = pl.BlockSpec((1, FFN_LT, D_MODEL), lambda b, j: (b, j, 0))
        out_shape = jax.ShapeDtypeStruct((BATCH, SEQ, D_MODEL), F32)
    return pl.pallas_call(
        functools.partial(_ffn_kernel, slab_out=slab_out),
        grid=(BATCH, SEQ // FFN_LT),
        in_specs=[
            pl.BlockSpec((1, FFN_LT, D_MODEL), lambda b, j: (b, j, 0)),
            _const_spec(w_up.shape), _const_spec(conv_w.shape), _const_spec(conv_b.shape),
            _const_spec(w_down.shape), _const_spec(gain.shape), _const_spec(bias.shape),
        ],
        out_specs=out_spec,
        out_shape=out_shape,
        scratch_shapes=[
            pltpu.VMEM((CONV_WIDTH - 1, 2 * D_FF), F32),
            pltpu.VMEM((SUBLANES + FFN_LT, FFN_FC), F32),
            pltpu.VMEM((FFN_LT, D_FF), BF16),
        ],
        compiler_params=_params("arbitrary", "arbitrary"),
        name="conv_glu_ffn_postnorm_slab" if slab_out else "conv_glu_ffn_postnorm",
    )(x, w_up, conv_w, conv_b, w_down, gain, bias)


def _qkv_kernel(h_ref, w_ref, o_ref, lhs_ref):
    g = pl.program_id(1)
    which = pl.program_id(2)

    for gi, dil in enumerate(DILATIONS):
        @pl.when(jnp.logical_and(g == gi, which == 0))
        def _(dil=dil):
            slots = SEQ // dil
            for r in range(dil):
                for k in range(N_SLABS):
                    if dil == 1:
                        rows = h_ref[0, k]
                    else:
                        rows = h_ref[0, k, pl.ds(r, slots, stride=dil), :]
                    lhs_ref[r * slots:(r + 1) * slots, k * LANES:(k + 1) * LANES] = rows.astype(BF16)

    scale = jnp.where(which == 0, HEAD_DIM ** -0.5, 1.0).astype(F32)
    for c in range(SEQ // PROJ_RC):
        rows = slice(c * PROJ_RC, (c + 1) * PROJ_RC)
        res = jnp.dot(lhs_ref[rows, :], w_ref[0, 0], preferred_element_type=F32)
        o_ref[0, 0, rows, :] = (res * scale).astype(BF16)


def _qkv_project(h_slabs, w_qkv):
    return pl.pallas_call(
        _qkv_kernel,
        grid=(BATCH, N_DIL, 3),
        in_specs=[
            pl.BlockSpec((1, N_SLABS, SEQ, LANES), lambda b, g, w: (b, 0, 0, 0)),
            pl.BlockSpec((1, 1, D_MODEL, D_MODEL), lambda b, g, w: (g, w, 0, 0)),
        ],
        out_specs=pl.BlockSpec((1, 1, SEQ, D_MODEL), lambda b, g, w: (b, g * 3 + w, 0, 0)),
        out_shape=jax.ShapeDtypeStruct((BATCH, 3 * N_DIL, SEQ, D_MODEL), BF16),
        scratch_shapes=[pltpu.VMEM((SEQ, D_MODEL), BF16)],
        compiler_params=_params("arbitrary", "arbitrary", "arbitrary"),
        name="qkv_project_regroup",
    )(h_slabs, w_qkv)


def _attn_kernel(qkv_ref, bias_ref, mask_ref, o_ref, og_ref, lse_ref, s_ref, p_ref, linv_ref):
    lane = lax.broadcasted_iota(jnp.int32, (BAND, LANES), 1)
    lower = lane < HEAD_DIM

    def run_group(g):
        dil = DILATIONS[g]
        n_blocks = SEQ // dil // BAND
        with_prev = n_blocks > 1
        n_keys = 2 * BAND if with_prev else BAND
        n_items = SEQ // BAND

        def geometry(idx):
            row0 = pl.multiple_of(idx * BAND, BAND)
            if with_prev:
                first = (idx % n_blocks) == 0
                prev0 = pl.multiple_of(jnp.where(first, row0, row0 - BAND), BAND)
                kind = first.astype(jnp.int32)
            else:
                prev0, kind = None, None
            start = idx // n_blocks + dil * BAND * (idx % n_blocks)
            return row0, prev0, kind, start

        def time_rows(start):
            return pl.ds(start, BAND, stride=dil) if dil > 1 else pl.ds(start, BAND)

        def keys_or_values(which, row0, prev0, cols):
            cur = qkv_ref[0, 3 * g + which, pl.ds(row0, BAND), cols]
            if not with_prev:
                return cur
            return jnp.concatenate([qkv_ref[0, 3 * g + which, pl.ds(prev0, BAND), cols], cur], axis=0)

        def scores(idx):
            row0, prev0, _, _ = geometry(idx)
            slot = idx % ATT_RING
            for slab in range(ATT_SLABS):
                cols = slice(slab * LANES, (slab + 1) * LANES)
                q2 = qkv_ref[0, 3 * g, pl.ds(row0, BAND), cols]
                zero = jnp.zeros_like(q2)
                qq = jnp.concatenate([jnp.where(lower, q2, zero), jnp.where(lower, zero, q2)], axis=0)
                kk = keys_or_values(1, row0, prev0, cols)
                s_ref[slot, slab, :, :n_keys] = lax.dot_general(
                    qq, kk, (((1,), (1,)), ((), ())), preferred_element_type=F32)

        def softmax(idx):
            _, _, kind, start = geometry(idx)
            slot = idx % ATT_RING
            if with_prev:
                valid = mask_ref[kind] != 0
            else:
                valid = mask_ref[0, :, BAND:] != 0
            for slab in range(ATT_SLABS):
                linv, lse = [], []
                for hh in range(2):
                    rows = slice(hh * BAND, (hh + 1) * BAND)
                    if with_prev:
                        bias = bias_ref[g, 2 * slab + hh]
                    else:
                        bias = bias_ref[g, 2 * slab + hh, :, BAND:]
                    s = jnp.where(valid, s_ref[slot, slab, rows, :n_keys] + bias, NEG_BIG)
                    m = jnp.max(s, axis=-1, keepdims=True)
                    p = jnp.exp(s - m)
                    l = jnp.sum(p, axis=-1, keepdims=True)
                    p_ref[slot, slab, rows, :n_keys] = p.astype(BF16)
                    linv.append(jnp.broadcast_to(1.0 / l, (BAND, LANES)))
                    lse.append(jnp.broadcast_to(m + jnp.log(l), (BAND, LANES)))
                linv_ref[slot, slab] = jnp.where(lower, linv[0], linv[1])
                lse_ref[g, slab, time_rows(start), :] = jnp.where(lower, lse[0], lse[1])

        def output(idx):
            row0, prev0, _, start = geometry(idx)
            slot = idx % ATT_RING
            for slab in range(ATT_SLABS):
                cols = slice(slab * LANES, (slab + 1) * LANES)
                vv = keys_or_values(2, row0, prev0, cols)
                pv = jnp.dot(p_ref[slot, slab, :, :n_keys], vv, preferred_element_type=F32)
                o0 = jnp.where(lower, pv[:BAND], pv[BAND:]) * linv_ref[slot, slab]
                if g > 0:
                    og_ref[g - 1, slab, time_rows(start), :] = o0
                else:
                    rows = pl.ds(row0, BAND)
                    l0, l1, l2 = lse_ref[0, slab, rows, :], lse_ref[1, slab, rows, :], lse_ref[2, slab, rows, :]
                    top = jnp.maximum(jnp.maximum(l0, l1), l2)
                    w0, w1, w2 = jnp.exp(l0 - top), jnp.exp(l1 - top), jnp.exp(l2 - top)
                    merged = (w0 * o0 + w1 * og_ref[0, slab, rows, :] + w2 * og_ref[1, slab, rows, :]) / (w0 + w1 + w2)
                    o_ref[0, rows, cols] = merged.astype(o_ref.dtype)

        stages = ((output, 2 * ATT_LAG), (softmax, ATT_LAG), (scores, 0))

        def run_steps(t0, n_steps, static):
            for stage, lag in stages:
                for j in range(n_steps):
                    idx = t0 + j - lag
                    if static:
                        if 0 <= idx < n_items:
                            stage(jnp.int32(idx))
                    else:
                        stage(idx)

        for t in range(2 * ATT_LAG):
            run_steps(t, 1, True)

        def steady(u, _):
            run_steps(2 * ATT_LAG + u * ATT_STEPS_PER_BODY, ATT_STEPS_PER_BODY, False)
            return 0
        lax.fori_loop(0, (n_items - 2 * ATT_LAG) // ATT_STEPS_PER_BODY, steady, 0)

        for t in range(n_items, n_items + 2 * ATT_LAG):
            run_steps(t, 1, True)

    for g in (2, 1, 0):
        run_group(g)


def _attention(qkv, bias, mask):
    return pl.pallas_call(
        _attn_kernel,
        grid=(BATCH, HEADS // HEADS_PER_STEP),
        in_specs=[
            pl.BlockSpec((1, 3 * N_DIL, SEQ, ATT_W), lambda b, c: (b, 0, 0, c)),
            pl.BlockSpec((N_DIL, HEADS_PER_STEP, BAND, 2 * BAND), lambda b, c: (0, c, 0, 0)),
            _const_spec(mask.shape),
        ],
        out_specs=pl.BlockSpec((1, SEQ, ATT_W), lambda b, c: (b, 0, c)),
        out_shape=jax.ShapeDtypeStruct((BATCH, SEQ, D_MODEL), BF16),
        scratch_shapes=[
            pltpu.VMEM((N_DIL - 1, ATT_SLABS, SEQ, LANES), F32),
            pltpu.VMEM((N_DIL, ATT_SLABS, SEQ, LANES), F32),
            pltpu.VMEM((ATT_RING, ATT_SLABS, 2 * BAND, 2 * BAND), F32),
            pltpu.VMEM((ATT_RING, ATT_SLABS, 2 * BAND, 2 * BAND), BF16),
            pltpu.VMEM((ATT_RING, ATT_SLABS, BAND, LANES), F32),
        ],
        compiler_params=_params("arbitrary", "arbitrary"),
        name="dilated_attention_merge",
    )(qkv, bias, mask)


def _outproj_kernel(a_ref, h_ref, w_ref, g_ref, b_ref, o_ref):
    res = jnp.concatenate([h_ref[0, k] for k in range(N_SLABS)], axis=-1)
    mix = jnp.dot(a_ref[0], w_ref[...], preferred_element_type=F32)
    o_ref[0] = _layer_norm(DN_ALPHA * res + mix, g_ref[...], b_ref[...])


def _outproj_layer(attn, h_slabs, w_out, gain, bias):
    return pl.pallas_call(
        _outproj_kernel,
        grid=(BATCH, SEQ // FFN_LT),
        in_specs=[
            pl.BlockSpec((1, FFN_LT, D_MODEL), lambda b, j: (b, j, 0)),
            pl.BlockSpec((1, N_SLABS, FFN_LT, LANES), lambda b, j: (b, 0, j, 0)),
            _const_spec(w_out.shape), _const_spec(gain.shape), _const_spec(bias.shape),
        ],
        out_specs=pl.BlockSpec((1, FFN_LT, D_MODEL), lambda b, j: (b, j, 0)),
        out_shape=jax.ShapeDtypeStruct((BATCH, SEQ, D_MODEL), F32),
        compiler_params=_params("arbitrary", "arbitrary"),
        name="attn_outproj_postnorm",
    )(attn, h_slabs, w_out, gain, bias)


def kernel(x, s5_lam_re, s5_lam_im, s5_log_dt, s5_b_re, s5_b_im, s5_c_re, s5_c_im, s5_d, s5_w_glu,
           s5_b_glu, s5_w_out, attn_w_kv, attn_w_q, attn_w_out, rel_bias, ffn_w_up, ffn_conv_w,
           ffn_conv_b, ffn_w_down, ln_gain, ln_bias):
    row = lambda v: v.astype(F32).reshape(1, -1)

    a, bb, cc = _s5_discretise(s5_lam_re[0], s5_lam_im[0], s5_log_dt[0], s5_b_re[0], s5_b_im[0],
                               s5_c_re[0], s5_c_im[0])
    h = _s5_layer(x, a, bb, cc, row(s5_d[0]), s5_w_glu[0].astype(BF16), row(s5_b_glu[0]),
                  s5_w_out[0].astype(BF16), row(ln_gain[0, 0]), row(ln_bias[0, 0]))
    h_slabs = _ffn_layer(h, ffn_w_up[0].astype(BF16), ffn_conv_w[0], row(ffn_conv_b[0]),
                         ffn_w_down[0].astype(BF16), row(ln_gain[0, 1]), row(ln_bias[0, 1]),
                         slab_out=True)

    split = lambda w: w.reshape(D_MODEL, N_DIL, D_MODEL)
    w_k, w_v = attn_w_kv[:, :N_DIL * D_MODEL], attn_w_kv[:, N_DIL * D_MODEL:]
    w_qkv = jnp.stack([split(attn_w_q[0]), split(w_k), split(w_v)], axis=0)
    w_qkv = jnp.transpose(w_qkv, (2, 0, 1, 3)).astype(BF16)
    qkv = _qkv_project(h_slabs, w_qkv)

    buckets, in_band = _bucket_tables()
    bias = _bias_tiles(rel_bias.astype(F32), jnp.asarray(buckets))
    attn = _attention(qkv, bias, jnp.asarray(in_band))
    h = _outproj_layer(attn, h_slabs, attn_w_out[0].astype(BF16), row(ln_gain[1, 0]), row(ln_bias[1, 0]))
    return _ffn_layer(h, ffn_w_up[1].astype(BF16), ffn_conv_w[1], row(ffn_conv_b[1]),
                      ffn_w_down[1].astype(BF16), row(ln_gain[1, 1]), row(ln_bias[1, 1]),
                      slab_out=False)
```

```python
import math

import numpy as np
import jax
import jax.numpy as jnp
from jax import lax
from jax.experimental import pallas as pl
from jax.experimental.pallas import tpu as pltpu

F32 = jnp.float32
BF16 = jnp.bfloat16

D_MODEL = 1024
BATCH = 16
SEQ = 2048
DEPTH = 2
SSM_GROUP = 16
SSM_GROUPS = D_MODEL // SSM_GROUP
SSM_STATE = 64
HEAD_DIM = 64
HEADS = D_MODEL // HEAD_DIM
DILATIONS = (1, 4, 16)
N_DIL = len(DILATIONS)
BAND = 128
NEG_BIG = -1e30
REL_BUCKETS = 32
REL_MAX_DIST = 2048
D_FF = 2816
CONV_WIDTH = 3
DN_ALPHA = (2.0 * DEPTH) ** 0.25
LN_EPS = 1e-5
LOG2_E = math.log2(math.e)

LANES = 128
SUBLANES = 8
N_SLABS = D_MODEL // LANES
VMEM_LIMIT = 56 * 1024 * 1024

S5_LC = 32
S5_OCT = SUBLANES
S5_ROWS = S5_LC * S5_OCT
N_PAIRS = SSM_GROUPS // 2
PAIR_W = 2 * LANES
STATE_W = N_PAIRS * PAIR_W
PAIRS_PER_SWEEP = N_PAIRS // N_SLABS
SWEEP_W = PAIRS_PER_SWEEP * PAIR_W
FFN_LT = 512
FFN_FC = 256
PROJ_RC = 512
HEADS_PER_STEP = 4
ATT_W = HEADS_PER_STEP * HEAD_DIM
ATT_SLABS = ATT_W // LANES
ATT_LAG = 2
ATT_STEPS_PER_BODY = 2
ATT_RING = 2 * ATT_LAG
assert ATT_STEPS_PER_BODY <= ATT_LAG


def _layer_norm(v, gain, bias):
    mu = jnp.mean(v, axis=-1, keepdims=True)
    d = v - mu
    var = jnp.mean(d * d, axis=-1, keepdims=True)
    return d * lax.rsqrt(var + LN_EPS) * gain + bias


def _const_spec(shape):
    n = len(shape)
    return pl.BlockSpec(shape, lambda *_: (0,) * n, pipeline_mode=pl.Buffered(1))


def _params(*sem):
    return pltpu.CompilerParams(dimension_semantics=sem, vmem_limit_bytes=VMEM_LIMIT)


def _t5_bucket(dist):
    exact = REL_BUCKETS // 2
    d = np.maximum(dist, 1).astype(np.float32)
    large = exact + (np.log(d / exact) / math.log(REL_MAX_DIST / exact)
                     * (REL_BUCKETS - exact)).astype(np.int64)
    large = np.minimum(large, REL_BUCKETS - 1)
    return np.where(dist < exact, dist, large).astype(np.int32)


def _bucket_tables():
    steps = np.arange(BAND)[:, None] + BAND - np.arange(2 * BAND)[None, :]
    buckets = np.stack([_t5_bucket(np.maximum(steps, 0) * d) for d in DILATIONS])
    in_band = ((steps >= 0) & (steps <= BAND)).astype(np.int32)
    return buckets.astype(np.int32), in_band


def _bias_kernel(rb_ref, bk_ref, band_ref, o_ref):
    g = pl.program_id(0)
    bk = bk_ref[0]
    in_band = band_ref[...] != 0
    for h in range(HEADS):
        acc = jnp.zeros((BAND, 2 * BAND), F32)
        for b in range(REL_BUCKETS):
            acc = jnp.where(bk == b, rb_ref[b, g * HEADS + h], acc)
        tile = jnp.where(in_band, acc * LOG2_E, -jnp.inf)
        o_ref[0, 0, h] = tile
        o_ref[0, 1, h, :, :BAND] = jnp.full((BAND, BAND), -jnp.inf, F32)
        o_ref[0, 1, h, :, BAND:] = tile[:, BAND:]


def _bias_tiles(rel_bias, buckets, in_band):
    return pl.pallas_call(
        _bias_kernel,
        grid=(N_DIL,),
        in_specs=[pl.BlockSpec(memory_space=pltpu.SMEM),
                  pl.BlockSpec((1, BAND, 2 * BAND), lambda g: (g, 0, 0)),
                  pl.BlockSpec((BAND, 2 * BAND), lambda g: (0, 0))],
        out_specs=pl.BlockSpec((1, 2, HEADS, BAND, 2 * BAND), lambda g: (g, 0, 0, 0, 0)),
        out_shape=jax.ShapeDtypeStruct((N_DIL, 2, HEADS, BAND, 2 * BAND), F32),
        compiler_params=_params("arbitrary"),
        name="rel_bias_tiles",
    )(rel_bias, buckets, in_band)


def _s5_kernel(x_ref, a_ref, bb_ref, cc_ref, d_ref, wglu_ref, bglu_ref, wout_ref, g_ref, b_ref,
               o_ref, ut_ref, st_ref, h_ref, yt_ref, ystd_ref):
    @pl.when(pl.program_id(1) == 0)
    def _():
        h_ref[...] = jnp.zeros_like(h_ref)

    for b in range(S5_OCT):
        for k in range(N_SLABS):
            ut_ref[k, pl.ds(b, S5_LC, stride=S5_OCT), :] = x_ref[b, :, k * LANES:(k + 1) * LANES]

    def project_in(s):
        cols = slice(s * SWEEP_W, (s + 1) * SWEEP_W)
        st_ref[:, cols] = jnp.dot(ut_ref[s].astype(BF16), bb_ref[s], preferred_element_type=F32)

    def recurrence(s):
        pairs = range(s * PAIRS_PER_SWEEP, (s + 1) * PAIRS_PER_SWEEP)
        ar = [a_ref[0, :, p * LANES:(p + 1) * LANES] for p in pairs]
        ai = [a_ref[1, :, p * LANES:(p + 1) * LANES] for p in pairs]
        hr = [h_ref[:, p * PAIR_W:p * PAIR_W + LANES] for p in pairs]
        hi = [h_ref[:, p * PAIR_W + LANES:(p + 1) * PAIR_W] for p in pairs]
        for t in range(S5_LC):
            rows = slice(t * S5_OCT, (t + 1) * S5_OCT)
            for q, p in enumerate(pairs):
                re_cols = slice(p * PAIR_W, p * PAIR_W + LANES)
                im_cols = slice(p * PAIR_W + LANES, (p + 1) * PAIR_W)
                new_r = ar[q] * hr[q] - ai[q] * hi[q] + st_ref[rows, re_cols]
                new_i = ar[q] * hi[q] + ai[q] * hr[q] + st_ref[rows, im_cols]
                st_ref[rows, re_cols] = new_r
                st_ref[rows, im_cols] = new_i
                hr[q], hi[q] = new_r, new_i
        for q, p in enumerate(pairs):
            h_ref[:, p * PAIR_W:p * PAIR_W + LANES] = hr[q]
            h_ref[:, p * PAIR_W + LANES:(p + 1) * PAIR_W] = hi[q]

    def project_out(s):
        lhs = st_ref[:, s * SWEEP_W:(s + 1) * SWEEP_W].astype(BF16)
        yt_ref[s] = jnp.dot(lhs, cc_ref[s], preferred_element_type=F32)

    project_in(0)
    for s in range(N_SLABS):
        if s + 1 < N_SLABS:
            project_in(s + 1)
        recurrence(s)
        project_out(s)

    for b in range(S5_OCT):
        yb = jnp.concatenate([yt_ref[k, pl.ds(b, S5_LC, stride=S5_OCT), :] for k in range(N_SLABS)], axis=-1)
        ystd_ref[b * S5_LC:(b + 1) * S5_LC, :] = jax.nn.gelu(yb + d_ref[...] * x_ref[b])

    y = ystd_ref[...]
    z = jnp.dot(y.astype(BF16), wglu_ref[...], preferred_element_type=F32) + bglu_ref[...]
    gated = y * jax.nn.sigmoid(z)
    mix = jnp.dot(gated.astype(BF16), wout_ref[...], preferred_element_type=F32)
    x_all = x_ref[...].reshape(S5_ROWS, D_MODEL)
    out = _layer_norm(DN_ALPHA * x_all + mix, g_ref[...], b_ref[...])
    o_ref[...] = out.reshape(S5_OCT, S5_LC, D_MODEL)


def _s5_discretise(lam_re, lam_im, log_dt, b_re, b_im, c_re, c_im):
    lr, li = lam_re.astype(F32), lam_im.astype(F32)
    dt = jnp.exp(log_dt.astype(F32))[:, None]
    mag = jnp.exp(lr * dt)
    ab_r, ab_i = mag * jnp.cos(li * dt), mag * jnp.sin(li * dt)
    den = lr * lr + li * li
    nr = ab_r - 1.0
    co_r = (nr * lr + ab_i * li) / den
    co_i = (ab_i * lr - nr * li) / den
    br, bi = b_re.astype(F32), b_im.astype(F32)
    bb_r = co_r[..., None] * br - co_i[..., None] * bi
    bb_i = co_r[..., None] * bi + co_i[..., None] * br

    eye2 = jnp.eye(2, dtype=F32)

    a = jnp.stack([ab_r.reshape(-1), ab_i.reshape(-1)])
    a = jnp.broadcast_to(a[:, None, :], (2, SUBLANES, SSM_GROUPS * SSM_STATE))

    def pair_blockdiag(m):
        g, r, c = m.shape
        m = m.reshape(g // 2, 2, r, c)
        out = jnp.einsum('pjrc,jk->pjrkc', m, eye2)
        return out.reshape(g // 2, 2 * r, 2 * c)

    wb = jnp.concatenate([pair_blockdiag(jnp.swapaxes(bb_r, 1, 2)),
                          pair_blockdiag(jnp.swapaxes(bb_i, 1, 2))], axis=-1)
    sel = jnp.asarray(np.eye(4, dtype=np.float32)[np.arange(N_PAIRS) % 4])
    bb = jnp.einsum('prn,pq->pqrn', wb, sel).reshape(N_SLABS, PAIRS_PER_SWEEP, LANES, PAIR_W)
    bb = jnp.swapaxes(bb, 1, 2).reshape(N_SLABS, LANES, SWEEP_W)

    wc = jnp.concatenate([pair_blockdiag(jnp.swapaxes(c_re.astype(F32), 1, 2)),
                          -pair_blockdiag(jnp.swapaxes(c_im.astype(F32), 1, 2))], axis=1)
    wc = wc.reshape(N_SLABS, 4, PAIR_W, 2 * SSM_GROUP)
    eye4 = jnp.eye(4, dtype=F32)
    cc = jnp.einsum('kqrc,qs->kqrsc', wc, eye4).reshape(N_SLABS, 4 * PAIR_W, LANES)
    return a, bb.astype(BF16), cc.astype(BF16)


def _s5_layer(x, a, bb, cc, d_skip, w_glu, b_glu, w_out, gain, bias):
    n_oct = BATCH // S5_OCT
    return pl.pallas_call(
        _s5_kernel,
        grid=(n_oct, SEQ // S5_LC),
        in_specs=[
            pl.BlockSpec((S5_OCT, S5_LC, D_MODEL), lambda o, c: (o, c, 0)),
            _const_spec(a.shape), _const_spec(bb.shape), _const_spec(cc.shape),
            _const_spec(d_skip.shape), _const_spec(w_glu.shape), _const_spec(b_glu.shape),
            _const_spec(w_out.shape), _const_spec(gain.shape), _const_spec(bias.shape),
        ],
        out_specs=pl.BlockSpec((S5_OCT, S5_LC, D_MODEL), lambda o, c: (o, c, 0)),
        out_shape=jax.ShapeDtypeStruct((BATCH, SEQ, D_MODEL), F32),
        scratch_shapes=[
            pltpu.VMEM((N_SLABS, S5_ROWS, LANES), F32),
            pltpu.VMEM((S5_ROWS, STATE_W), F32),
            pltpu.VMEM((S5_OCT, STATE_W), F32),
            pltpu.VMEM((N_SLABS, S5_ROWS, LANES), F32),
            pltpu.VMEM((S5_ROWS, D_MODEL), F32),
        ],
        compiler_params=_params("arbitrary", "arbitrary"),
        name="s5_glu_postnorm",
    )(x, a, bb, cc, d_skip, w_glu, b_glu, w_out, gain, bias)


def _ffn_kernel(x_ref, wup_ref, cw_ref, cb_ref, wdown_ref, g_ref, b_ref, o_ref,
                carry_ref, pad_ref, act_ref):
    _ffn_block(x_ref[0], wup_ref, cw_ref, cb_ref, wdown_ref, g_ref, b_ref, o_ref,
               carry_ref, pad_ref, act_ref, slab_out=True)


def _outproj_ffn_kernel(a_ref, h_ref, wo_ref, g0_ref, b0_ref, wup_ref, cw_ref, cb_ref, wdown_ref,
                        g_ref, b_ref, o_ref, carry_ref, pad_ref, act_ref):
    res = jnp.concatenate([h_ref[0, k] for k in range(N_SLABS)], axis=-1)
    mix = jnp.dot(a_ref[0], wo_ref[...], preferred_element_type=F32)
    x = _layer_norm(DN_ALPHA * res + mix, g0_ref[...], b0_ref[...])
    _ffn_block(x, wup_ref, cw_ref, cb_ref, wdown_ref, g_ref, b_ref, o_ref,
               carry_ref, pad_ref, act_ref, slab_out=False)


def _ffn_block(x, wup_ref, cw_ref, cb_ref, wdown_ref, g_ref, b_ref, o_ref,
               carry_ref, pad_ref, act_ref, *, slab_out):
    @pl.when(pl.program_id(1) == 0)
    def _():
        carry_ref[...] = jnp.zeros_like(carry_ref)

    xb = x.astype(BF16)
    halo = SUBLANES
    for c in range(D_FF // FFN_FC):
        conv = []
        for part in range(2):
            cols = slice(part * D_FF + c * FFN_FC, part * D_FF + (c + 1) * FFN_FC)
            up = jnp.dot(xb, wup_ref[:, cols], preferred_element_type=F32)
            pad_ref[halo - 2:halo, :] = carry_ref[:, cols]
            pad_ref[halo:halo + FFN_LT, :] = up
            carry_ref[:, cols] = up[FFN_LT - 2:, :]
            back1 = pad_ref[halo - 1:halo - 1 + FFN_LT, :]
            back2 = pad_ref[halo - 2:halo - 2 + FFN_LT, :]
            conv.append(cb_ref[:, cols] + cw_ref[0:1, cols] * up + cw_ref[1:2, cols] * back1
                        + cw_ref[2:3, cols] * back2)
        val, gate = conv
        act_ref[:, c * FFN_FC:(c + 1) * FFN_FC] = (jax.nn.silu(gate) * val).astype(BF16)

    ffn = jnp.dot(act_ref[...], wdown_ref[...], preferred_element_type=F32)
    out = _layer_norm(DN_ALPHA * x + ffn, g_ref[...], b_ref[...])
    if slab_out:
        for k in range(N_SLABS):
            o_ref[0, k] = out[:, k * LANES:(k + 1) * LANES]
    else:
        o_ref[0] = out


_ROWS_SPEC = pl.BlockSpec((1, FFN_LT, D_MODEL), lambda b, j: (b, j, 0))
_SLAB_ROWS_SPEC = pl.BlockSpec((1, N_SLABS, FFN_LT, LANES), lambda b, j: (b, 0, j, 0))
_FFN_SCRATCH = [
    pltpu.VMEM((CONV_WIDTH - 1, 2 * D_FF), F32),
    pltpu.VMEM((SUBLANES + FFN_LT, FFN_FC), F32),
    pltpu.VMEM((FFN_LT, D_FF), BF16),
]


def _ffn_layer(x, ffn):
    return pl.pallas_call(
        _ffn_kernel,
        grid=(BATCH, SEQ // FFN_LT),
        in_specs=[_ROWS_SPEC] + [_const_spec(w.shape) for w in ffn],
        out_specs=_SLAB_ROWS_SPEC,
        out_shape=jax.ShapeDtypeStruct((BATCH, N_SLABS, SEQ, LANES), F32),
        scratch_shapes=_FFN_SCRATCH,
        compiler_params=_params("arbitrary", "arbitrary"),
        name="conv_glu_ffn_postnorm_slab",
    )(x, *ffn)


def _outproj_ffn_layer(attn, h_slabs, outproj, ffn):
    return pl.pallas_call(
        _outproj_ffn_kernel,
        grid=(BATCH, SEQ // FFN_LT),
        in_specs=[_ROWS_SPEC, _SLAB_ROWS_SPEC] + [_const_spec(w.shape) for w in outproj + ffn],
        out_specs=_ROWS_SPEC,
        out_shape=jax.ShapeDtypeStruct((BATCH, SEQ, D_MODEL), F32),
        scratch_shapes=_FFN_SCRATCH,
        compiler_params=_params("arbitrary", "arbitrary"),
        name="attn_outproj_ffn_postnorm",
    )(attn, h_slabs, *outproj, *ffn)


def _qkv_kernel(h_ref, w_ref, o_ref, lhs_ref):
    g = pl.program_id(1)
    which = pl.program_id(2)

    for gi, dil in enumerate(DILATIONS):
        @pl.when(jnp.logical_and(g == gi, which == 0))
        def _(dil=dil):
            slots = SEQ // dil
            for r in range(dil):
                for k in range(N_SLABS):
                    if dil == 1:
                        rows = h_ref[0, k]
                    else:
                        rows = h_ref[0, k, pl.ds(r, slots, stride=dil), :]
                    lhs_ref[r * slots:(r + 1) * slots, k * LANES:(k + 1) * LANES] = rows.astype(BF16)

    scale = jnp.where(which == 0, HEAD_DIM ** -0.5 * LOG2_E, 1.0).astype(F32)
    for c in range(SEQ // PROJ_RC):
        rows = slice(c * PROJ_RC, (c + 1) * PROJ_RC)
        res = jnp.dot(lhs_ref[rows, :], w_ref[0, 0], preferred_element_type=F32)
        o_ref[0, 0, rows, :] = (res * scale).astype(BF16)


def _qkv_project(h_slabs, w_qkv):
    return pl.pallas_call(
        _qkv_kernel,
        grid=(BATCH, N_DIL, 3),
        in_specs=[
            pl.BlockSpec((1, N_SLABS, SEQ, LANES), lambda b, g, w: (b, 0, 0, 0)),
            pl.BlockSpec((1, 1, D_MODEL, D_MODEL), lambda b, g, w: (g, w, 0, 0)),
        ],
        out_specs=pl.BlockSpec((1, 1, SEQ, D_MODEL), lambda b, g, w: (b, g * 3 + w, 0, 0)),
        out_shape=jax.ShapeDtypeStruct((BATCH, 3 * N_DIL, SEQ, D_MODEL), BF16),
        scratch_shapes=[pltpu.VMEM((SEQ, D_MODEL), BF16)],
        compiler_params=_params("arbitrary", "arbitrary", "arbitrary"),
        name="qkv_project_regroup",
    )(h_slabs, w_qkv)


def _attn_kernel(qkv_ref, bias_ref, ones_ref, o_ref, og_ref, lse_ref, s_ref, p_ref, top_ref):
    lane = lax.broadcasted_iota(jnp.int32, (BAND, LANES), 1)
    lower = lane < HEAD_DIM

    def run_group(g):
        dil = DILATIONS[g]
        n_blocks = SEQ // dil // BAND
        with_prev = n_blocks > 1
        n_keys = 2 * BAND if with_prev else BAND
        n_items = SEQ // BAND
        lower_k = lax.broadcasted_iota(jnp.int32, (n_keys, LANES), 1) < HEAD_DIM

        def geometry(idx):
            row0 = pl.multiple_of(idx * BAND, BAND)
            if with_prev:
                first = jnp.asarray((idx % n_blocks) == 0)
                prev0 = pl.multiple_of(jnp.where(first, row0, row0 - BAND), BAND)
                kind = first.astype(jnp.int32)
            else:
                prev0, kind = None, None
            start = idx // n_blocks + dil * BAND * (idx % n_blocks)
            return row0, prev0, kind, start

        def time_rows(start):
            return pl.ds(start, BAND, stride=dil) if dil > 1 else pl.ds(start, BAND)

        def keys_or_values(which, row0, prev0, cols):
            cur = qkv_ref[0, 3 * g + which, pl.ds(row0, BAND), cols]
            if not with_prev:
                return cur
            return jnp.concatenate([qkv_ref[0, 3 * g + which, pl.ds(prev0, BAND), cols], cur], axis=0)

        def scores(idx):
            row0, prev0, _, _ = geometry(idx)
            slot = idx % ATT_RING
            for slab in range(ATT_SLABS):
                cols = slice(slab * LANES, (slab + 1) * LANES)
                q2 = qkv_ref[0, 3 * g, pl.ds(row0, BAND), cols]
                zero = jnp.zeros_like(q2)
                qq = jnp.concatenate([jnp.where(lower, q2, zero), jnp.where(lower, zero, q2)], axis=0)
                kk = keys_or_values(1, row0, prev0, cols)
                s_ref[slot, slab, :, :n_keys] = lax.dot_general(
                    qq, kk, (((1,), (1,)), ((), ())), preferred_element_type=F32)

        def softmax(idx):
            _, _, kind, _ = geometry(idx)
            slot = idx % ATT_RING
            for slab in range(ATT_SLABS):
                top = []
                for hh in range(2):
                    rows = slice(hh * BAND, (hh + 1) * BAND)
                    if with_prev:
                        bias = bias_ref[g, kind, 2 * slab + hh]
                    else:
                        bias = bias_ref[g, 0, 2 * slab + hh, :, BAND:]
                    s = s_ref[slot, slab, rows, :n_keys] + bias
                    m = jnp.max(s, axis=-1, keepdims=True)
                    p_ref[slot, slab, :, hh * n_keys:(hh + 1) * n_keys] = jnp.exp2(s - m).astype(BF16)
                    top.append(jnp.broadcast_to(m, (BAND, LANES)))
                top_ref[slot, slab] = jnp.where(lower, top[0], top[1])

        def output(idx):
            row0, prev0, _, start = geometry(idx)
            slot = idx % ATT_RING
            for slab in range(ATT_SLABS):
                cols = slice(slab * LANES, (slab + 1) * LANES)
                vv = keys_or_values(2, row0, prev0, cols)
                zero = jnp.zeros_like(vv)
                rhs = jnp.concatenate([
                    jnp.concatenate([jnp.where(lower_k, vv, zero), ones_ref[0, :n_keys, :]], axis=1),
                    jnp.concatenate([jnp.where(lower_k, zero, vv), ones_ref[1, :n_keys, :]], axis=1),
                ], axis=0)
                res = jnp.dot(p_ref[slot, slab, :, :2 * n_keys], rhs, preferred_element_type=F32)
                den = res[:, LANES:]
                o0 = res[:, :LANES] * (1.0 / den)
                l_own = top_ref[slot, slab] + jnp.log2(den)
                if g > 0:
                    og_ref[g - 1, slab, time_rows(start), :] = o0
                    lse_ref[g - 1, slab, time_rows(start), :] = l_own
                else:
                    rows = pl.ds(row0, BAND)
                    l0, l1, l2 = l_own, lse_ref[0, slab, rows, :], lse_ref[1, slab, rows, :]
                    top = jnp.maximum(jnp.maximum(l0, l1), l2)
                    w0, w1, w2 = jnp.exp2(l0 - top), jnp.exp2(l1 - top), jnp.exp2(l2 - top)
                    merged = (w0 * o0 + w1 * og_ref[0, slab, rows, :] + w2 * og_ref[1, slab, rows, :]) / (w0 + w1 + w2)
                    o_ref[0, rows, cols] = merged.astype(o_ref.dtype)

        stages = ((output, 2 * ATT_LAG), (softmax, ATT_LAG), (scores, 0))

        def run_steps(t0, n_steps, static):
            for stage, lag in stages:
                for j in range(n_steps):
                    idx = t0 + j - lag
                    if static:
                        if 0 <= idx < n_items:
                            stage(jnp.int32(idx))
                    else:
                        stage(idx)

        for t in range(2 * ATT_LAG):
            run_steps(t, 1, True)

        def steady(u, _):
            run_steps(2 * ATT_LAG + u * ATT_STEPS_PER_BODY, ATT_STEPS_PER_BODY, False)
            return 0
        lax.fori_loop(0, (n_items - 2 * ATT_LAG) // ATT_STEPS_PER_BODY, steady, 0)

        for t in range(n_items, n_items + 2 * ATT_LAG):
            run_steps(t, 1, True)

    for g in (2, 1, 0):
        run_group(g)


def _attention(qkv, bias):
    head_of_lane = np.arange(LANES) // HEAD_DIM
    ones = np.stack([np.broadcast_to(head_of_lane == hh, (2 * BAND, LANES)) for hh in range(2)])
    ones = jnp.asarray(ones, BF16)
    return pl.pallas_call(
        _attn_kernel,
        grid=(BATCH, HEADS // HEADS_PER_STEP),
        in_specs=[
            pl.BlockSpec((1, 3 * N_DIL, SEQ, ATT_W), lambda b, c: (b, 0, 0, c)),
            pl.BlockSpec((N_DIL, 2, HEADS_PER_STEP, BAND, 2 * BAND), lambda b, c: (0, 0, c, 0, 0)),
            _const_spec(ones.shape),
        ],
        out_specs=pl.BlockSpec((1, SEQ, ATT_W), lambda b, c: (b, 0, c)),
        out_shape=jax.ShapeDtypeStruct((BATCH, SEQ, D_MODEL), BF16),
        scratch_shapes=[
            pltpu.VMEM((N_DIL - 1, ATT_SLABS, SEQ, LANES), F32),
            pltpu.VMEM((N_DIL - 1, ATT_SLABS, SEQ, LANES), F32),
            pltpu.VMEM((ATT_RING, ATT_SLABS, 2 * BAND, 2 * BAND), F32),
            pltpu.VMEM((ATT_RING, ATT_SLABS, BAND, 4 * BAND), BF16),
            pltpu.VMEM((ATT_RING, ATT_SLABS, BAND, LANES), F32),
        ],
        compiler_params=_params("arbitrary", "arbitrary"),
        name="dilated_attention_merge",
    )(qkv, bias, ones)


def kernel(x, s5_lam_re, s5_lam_im, s5_log_dt, s5_b_re, s5_b_im, s5_c_re, s5_c_im, s5_d, s5_w_glu,
           s5_b_glu, s5_w_out, attn_w_kv, attn_w_q, attn_w_out, rel_bias, ffn_w_up, ffn_conv_w,
           ffn_conv_b, ffn_w_down, ln_gain, ln_bias):
    row = lambda v: v.astype(F32).reshape(1, -1)

    a, bb, cc = _s5_discretise(s5_lam_re[0], s5_lam_im[0], s5_log_dt[0], s5_b_re[0], s5_b_im[0],
                               s5_c_re[0], s5_c_im[0])
    h = _s5_layer(x, a, bb, cc, row(s5_d[0]), s5_w_glu[0].astype(BF16), row(s5_b_glu[0]),
                  s5_w_out[0].astype(BF16), row(ln_gain[0, 0]), row(ln_bias[0, 0]))
    ffn = lambda i: [ffn_w_up[i].astype(BF16), ffn_conv_w[i].astype(F32), row(ffn_conv_b[i]),
                     ffn_w_down[i].astype(BF16), row(ln_gain[i, 1]), row(ln_bias[i, 1])]
    h_slabs = _ffn_layer(h, ffn(0))

    split = lambda w: w.reshape(D_MODEL, N_DIL, D_MODEL)
    w_k, w_v = attn_w_kv[:, :N_DIL * D_MODEL], attn_w_kv[:, N_DIL * D_MODEL:]
    w_qkv = jnp.stack([split(attn_w_q[0]), split(w_k), split(w_v)], axis=0)
    w_qkv = jnp.transpose(w_qkv, (2, 0, 1, 3)).astype(BF16)
    qkv = _qkv_project(h_slabs, w_qkv)

    buckets, in_band = _bucket_tables()
    bias = _bias_tiles(rel_bias.astype(F32), jnp.asarray(buckets), jnp.asarray(in_band))
    attn = _attention(qkv, bias)
    outproj = [attn_w_out[0].astype(BF16), row(ln_gain[1, 0]), row(ln_bias[1, 0])]
    return _outproj_ffn_layer(attn, h_slabs, outproj, ffn(1))
```

```python
import math

import numpy as np
import jax
import jax.numpy as jnp
from jax import lax
from jax.experimental import pallas as pl
from jax.experimental.pallas import tpu as pltpu

F32 = jnp.float32
BF16 = jnp.bfloat16

D_MODEL = 1024
BATCH = 16
SEQ = 2048
DEPTH = 2
SSM_GROUP = 16
SSM_GROUPS = D_MODEL // SSM_GROUP
SSM_STATE = 64
HEAD_DIM = 64
HEADS = D_MODEL // HEAD_DIM
DILATIONS = (1, 4, 16)
N_DIL = len(DILATIONS)
BAND = 128
NEG_BIG = -1e30
REL_BUCKETS = 32
REL_MAX_DIST = 2048
D_FF = 2816
CONV_WIDTH = 3
DN_ALPHA = (2.0 * DEPTH) ** 0.25
LN_EPS = 1e-5
LOG2_E = math.log2(math.e)

LANES = 128
SUBLANES = 8
N_SLABS = D_MODEL // LANES
VMEM_LIMIT = 56 * 1024 * 1024

S5_LC = 32
S5_OCT = SUBLANES
S5_ROWS = S5_LC * S5_OCT
N_PAIRS = SSM_GROUPS // 2
PAIR_W = 2 * LANES
STATE_W = N_PAIRS * PAIR_W
PAIRS_PER_SWEEP = N_PAIRS // N_SLABS
SWEEP_W = PAIRS_PER_SWEEP * PAIR_W
FFN_LT = 512
FFN_FC = 256
PROJ_RC = 512
HEADS_PER_STEP = 4
ATT_W = HEADS_PER_STEP * HEAD_DIM
ATT_SLABS = ATT_W // LANES
ATT_LAG = 2
ATT_STEPS_PER_BODY = 2
ATT_RING = 2 * ATT_LAG
assert ATT_STEPS_PER_BODY <= ATT_LAG


def _layer_norm(v, gain, bias):
    mu = jnp.mean(v, axis=-1, keepdims=True)
    d = v - mu
    var = jnp.mean(d * d, axis=-1, keepdims=True)
    return d * lax.rsqrt(var + LN_EPS) * gain + bias


def _const_spec(shape):
    n = len(shape)
    return pl.BlockSpec(shape, lambda *_: (0,) * n, pipeline_mode=pl.Buffered(1))


def _layer_spec(shape, layer):
    n = len(shape)
    return pl.BlockSpec((None,) + tuple(shape[1:]), lambda *_: (layer,) + (0,) * (n - 1),
                        pipeline_mode=pl.Buffered(1))


def _params(*sem):
    return pltpu.CompilerParams(dimension_semantics=sem, vmem_limit_bytes=VMEM_LIMIT)


def _t5_bucket(dist):
    exact = REL_BUCKETS // 2
    d = np.maximum(dist, 1).astype(np.float32)
    large = exact + (np.log(d / exact) / math.log(REL_MAX_DIST / exact)
                     * (REL_BUCKETS - exact)).astype(np.int64)
    large = np.minimum(large, REL_BUCKETS - 1)
    return np.where(dist < exact, dist, large).astype(np.int32)


def _bucket_tables():
    steps = np.arange(BAND)[:, None] + BAND - np.arange(2 * BAND)[None, :]
    buckets = np.stack([_t5_bucket(np.maximum(steps, 0) * d) for d in DILATIONS])
    in_band = ((steps >= 0) & (steps <= BAND)).astype(np.int32)
    return buckets.astype(np.int32), in_band


def _bias_kernel(rb_ref, bk_ref, band_ref, o_ref):
    g = pl.program_id(0)
    bk = bk_ref[0]
    in_band = band_ref[...] != 0
    for h in range(HEADS):
        acc = jnp.zeros((BAND, 2 * BAND), F32)
        for b in range(REL_BUCKETS):
            acc = jnp.where(bk == b, rb_ref[b, g * HEADS + h], acc)
        tile = jnp.where(in_band, acc * LOG2_E, -jnp.inf)
        o_ref[0, 0, h] = tile
        o_ref[0, 1, h, :, :BAND] = jnp.full((BAND, BAND), -jnp.inf, F32)
        o_ref[0, 1, h, :, BAND:] = tile[:, BAND:]


def _bias_tiles(rel_bias, buckets, in_band):
    return pl.pallas_call(
        _bias_kernel,
        grid=(N_DIL,),
        in_specs=[pl.BlockSpec(memory_space=pltpu.SMEM),
                  pl.BlockSpec((1, BAND, 2 * BAND), lambda g: (g, 0, 0)),
                  pl.BlockSpec((BAND, 2 * BAND), lambda g: (0, 0))],
        out_specs=pl.BlockSpec((1, 2, HEADS, BAND, 2 * BAND), lambda g: (g, 0, 0, 0, 0)),
        out_shape=jax.ShapeDtypeStruct((N_DIL, 2, HEADS, BAND, 2 * BAND), F32),
        compiler_params=_params("arbitrary"),
        name="rel_bias_tiles",
    )(rel_bias, buckets, in_band)


def _s5_kernel(x_ref, a_ref, bb_ref, cc_ref, d_ref, wglu_ref, bglu_ref, wout_ref, g_ref, b_ref,
               o_ref, ut_ref, st_ref, h_ref, yt_ref, ystd_ref):
    @pl.when(pl.program_id(1) == 0)
    def _():
        h_ref[...] = jnp.zeros_like(h_ref)

    for b in range(S5_OCT):
        for k in range(N_SLABS):
            ut_ref[k, pl.ds(b, S5_LC, stride=S5_OCT), :] = x_ref[b, :, k * LANES:(k + 1) * LANES]

    def project_in(s):
        cols = slice(s * SWEEP_W, (s + 1) * SWEEP_W)
        st_ref[:, cols] = jnp.dot(ut_ref[s].astype(BF16), bb_ref[s], preferred_element_type=F32)

    def recurrence(s):
        pairs = range(s * PAIRS_PER_SWEEP, (s + 1) * PAIRS_PER_SWEEP)
        ar = [a_ref[0, :, p * LANES:(p + 1) * LANES] for p in pairs]
        ai = [a_ref[1, :, p * LANES:(p + 1) * LANES] for p in pairs]
        hr = [h_ref[:, p * PAIR_W:p * PAIR_W + LANES] for p in pairs]
        hi = [h_ref[:, p * PAIR_W + LANES:(p + 1) * PAIR_W] for p in pairs]
        for t in range(S5_LC):
            rows = slice(t * S5_OCT, (t + 1) * S5_OCT)
            for q, p in enumerate(pairs):
                re_cols = slice(p * PAIR_W, p * PAIR_W + LANES)
                im_cols = slice(p * PAIR_W + LANES, (p + 1) * PAIR_W)
                new_r = ar[q] * hr[q] - ai[q] * hi[q] + st_ref[rows, re_cols]
                new_i = ar[q] * hi[q] + ai[q] * hr[q] + st_ref[rows, im_cols]
                st_ref[rows, re_cols] = new_r
                st_ref[rows, im_cols] = new_i
                hr[q], hi[q] = new_r, new_i
        for q, p in enumerate(pairs):
            h_ref[:, p * PAIR_W:p * PAIR_W + LANES] = hr[q]
            h_ref[:, p * PAIR_W + LANES:(p + 1) * PAIR_W] = hi[q]

    def project_out(s):
        lhs = st_ref[:, s * SWEEP_W:(s + 1) * SWEEP_W].astype(BF16)
        yt_ref[s] = jnp.dot(lhs, cc_ref[s], preferred_element_type=F32)

    project_in(0)
    for s in range(N_SLABS):
        if s + 1 < N_SLABS:
            project_in(s + 1)
        recurrence(s)
        project_out(s)

    for b in range(S5_OCT):
        yb = jnp.concatenate([yt_ref[k, pl.ds(b, S5_LC, stride=S5_OCT), :] for k in range(N_SLABS)], axis=-1)
        ystd_ref[b * S5_LC:(b + 1) * S5_LC, :] = jax.nn.gelu(yb + d_ref[...] * x_ref[b])

    y = ystd_ref[...]
    z = jnp.dot(y.astype(BF16), wglu_ref[...], preferred_element_type=F32) + bglu_ref[...]
    gated = y * jax.nn.sigmoid(z)
    mix = jnp.dot(gated.astype(BF16), wout_ref[...], preferred_element_type=F32)
    x_all = x_ref[...].reshape(S5_ROWS, D_MODEL)
    out = _layer_norm(DN_ALPHA * x_all + mix, g_ref[...], b_ref[...])
    o_ref[...] = out.reshape(S5_OCT, S5_LC, D_MODEL)


def _s5_discretise(lam_re, lam_im, log_dt, b_re, b_im, c_re, c_im):
    lr, li = lam_re.astype(F32), lam_im.astype(F32)
    dt = jnp.exp(log_dt.astype(F32))[:, None]
    mag = jnp.exp(lr * dt)
    ab_r, ab_i = mag * jnp.cos(li * dt), mag * jnp.sin(li * dt)
    den = lr * lr + li * li
    nr = ab_r - 1.0
    co_r = (nr * lr + ab_i * li) / den
    co_i = (ab_i * lr - nr * li) / den
    br, bi = b_re.astype(F32), b_im.astype(F32)
    bb_r = co_r[..., None] * br - co_i[..., None] * bi
    bb_i = co_r[..., None] * bi + co_i[..., None] * br

    eye2 = jnp.eye(2, dtype=F32)

    a = jnp.stack([ab_r.reshape(-1), ab_i.reshape(-1)])
    a = jnp.broadcast_to(a[:, None, :], (2, SUBLANES, SSM_GROUPS * SSM_STATE))

    def pair_blockdiag(m):
        g, r, c = m.shape
        m = m.reshape(g // 2, 2, r, c)
        out = jnp.einsum('pjrc,jk->pjrkc', m, eye2)
        return out.reshape(g // 2, 2 * r, 2 * c)

    wb = jnp.concatenate([pair_blockdiag(jnp.swapaxes(bb_r, 1, 2)),
                          pair_blockdiag(jnp.swapaxes(bb_i, 1, 2))], axis=-1)
    sel = jnp.asarray(np.eye(4, dtype=np.float32)[np.arange(N_PAIRS) % 4])
    bb = jnp.einsum('prn,pq->pqrn', wb, sel).reshape(N_SLABS, PAIRS_PER_SWEEP, LANES, PAIR_W)
    bb = jnp.swapaxes(bb, 1, 2).reshape(N_SLABS, LANES, SWEEP_W)

    wc = jnp.concatenate([pair_blockdiag(jnp.swapaxes(c_re.astype(F32), 1, 2)),
                          -pair_blockdiag(jnp.swapaxes(c_im.astype(F32), 1, 2))], axis=1)
    wc = wc.reshape(N_SLABS, 4, PAIR_W, 2 * SSM_GROUP)
    eye4 = jnp.eye(4, dtype=F32)
    cc = jnp.einsum('kqrc,qs->kqrsc', wc, eye4).reshape(N_SLABS, 4 * PAIR_W, LANES)
    return a, bb.astype(BF16), cc.astype(BF16)


def _s5_layer(x, a, bb, cc, stacked):
    n_oct = BATCH // S5_OCT
    return pl.pallas_call(
        _s5_kernel,
        grid=(n_oct, SEQ // S5_LC),
        in_specs=[
            pl.BlockSpec((S5_OCT, S5_LC, D_MODEL), lambda o, c: (o, c, 0)),
            _const_spec(a.shape), _const_spec(bb.shape), _const_spec(cc.shape),
        ] + [_layer_spec(w.shape, i) for w, i in stacked],
        out_specs=pl.BlockSpec((S5_OCT, S5_LC, D_MODEL), lambda o, c: (o, c, 0)),
        out_shape=jax.ShapeDtypeStruct((BATCH, SEQ, D_MODEL), F32),
        scratch_shapes=[
            pltpu.VMEM((N_SLABS, S5_ROWS, LANES), F32),
            pltpu.VMEM((S5_ROWS, STATE_W), F32),
            pltpu.VMEM((S5_OCT, STATE_W), F32),
            pltpu.VMEM((N_SLABS, S5_ROWS, LANES), F32),
            pltpu.VMEM((S5_ROWS, D_MODEL), F32),
        ],
        compiler_params=_params("arbitrary", "arbitrary"),
        name="s5_glu_postnorm",
    )(x, a, bb, cc, *[w for w, _ in stacked])


def _ffn_kernel(x_ref, wup_ref, cw_ref, cb_ref, wdown_ref, g_ref, b_ref, o_ref,
                carry_ref, pad_ref, act_ref):
    _ffn_block(x_ref[0], wup_ref, cw_ref, cb_ref, wdown_ref, g_ref, b_ref, o_ref,
               carry_ref, pad_ref, act_ref, slab_out=True)


def _outproj_ffn_kernel(a_ref, h_ref, wo_ref, g0_ref, b0_ref, wup_ref, cw_ref, cb_ref, wdown_ref,
                        g_ref, b_ref, o_ref, carry_ref, pad_ref, act_ref):
    res = jnp.concatenate([h_ref[0, k] for k in range(N_SLABS)], axis=-1)
    mix = jnp.dot(a_ref[0], wo_ref[...], preferred_element_type=F32)
    x = _layer_norm(DN_ALPHA * res + mix, g0_ref[...], b0_ref[...])
    _ffn_block(x, wup_ref, cw_ref, cb_ref, wdown_ref, g_ref, b_ref, o_ref,
               carry_ref, pad_ref, act_ref, slab_out=False)


def _ffn_block(x, wup_ref, cw_ref, cb_ref, wdown_ref, g_ref, b_ref, o_ref,
               carry_ref, pad_ref, act_ref, *, slab_out):
    @pl.when(pl.program_id(1) == 0)
    def _():
        carry_ref[...] = jnp.zeros_like(carry_ref)

    xb = x.astype(BF16)
    halo = SUBLANES
    for c in range(D_FF // FFN_FC):
        conv = []
        for part in range(2):
            cols = slice(part * D_FF + c * FFN_FC, part * D_FF + (c + 1) * FFN_FC)
            up = jnp.dot(xb, wup_ref[:, cols], preferred_element_type=F32)
            pad_ref[halo - 2:halo, :] = carry_ref[:, cols]
            pad_ref[halo:halo + FFN_LT, :] = up
            carry_ref[:, cols] = up[FFN_LT - 2:, :]
            back1 = pad_ref[halo - 1:halo - 1 + FFN_LT, :]
            back2 = pad_ref[halo - 2:halo - 2 + FFN_LT, :]
            conv.append(cb_ref[:, cols] + cw_ref[0:1, cols] * up + cw_ref[1:2, cols] * back1
                        + cw_ref[2:3, cols] * back2)
        val, gate = conv
        act_ref[:, c * FFN_FC:(c + 1) * FFN_FC] = (jax.nn.silu(gate) * val).astype(BF16)

    ffn = jnp.dot(act_ref[...], wdown_ref[...], preferred_element_type=F32)
    out = _layer_norm(DN_ALPHA * x + ffn, g_ref[...], b_ref[...])
    if slab_out:
        for k in range(N_SLABS):
            o_ref[0, k] = out[:, k * LANES:(k + 1) * LANES]
    else:
        o_ref[0] = out


_ROWS_SPEC = pl.BlockSpec((1, FFN_LT, D_MODEL), lambda b, j: (b, j, 0))
_SLAB_ROWS_SPEC = pl.BlockSpec((1, N_SLABS, FFN_LT, LANES), lambda b, j: (b, 0, j, 0))
_FFN_SCRATCH = [
    pltpu.VMEM((CONV_WIDTH - 1, 2 * D_FF), F32),
    pltpu.VMEM((SUBLANES + FFN_LT, FFN_FC), F32),
    pltpu.VMEM((FFN_LT, D_FF), BF16),
]


def _ffn_layer(x, ffn):
    return pl.pallas_call(
        _ffn_kernel,
        grid=(BATCH, SEQ // FFN_LT),
        in_specs=[_ROWS_SPEC] + [_layer_spec(w.shape, i) for w, i in ffn],
        out_specs=_SLAB_ROWS_SPEC,
        out_shape=jax.ShapeDtypeStruct((BATCH, N_SLABS, SEQ, LANES), F32),
        scratch_shapes=_FFN_SCRATCH,
        compiler_params=_params("arbitrary", "arbitrary"),
        name="conv_glu_ffn_postnorm_slab",
    )(x, *[w for w, _ in ffn])


def _outproj_ffn_layer(attn, h_slabs, outproj, ffn):
    return pl.pallas_call(
        _outproj_ffn_kernel,
        grid=(BATCH, SEQ // FFN_LT),
        in_specs=[_ROWS_SPEC, _SLAB_ROWS_SPEC] + [_layer_spec(w.shape, i) for w, i in outproj + ffn],
        out_specs=_ROWS_SPEC,
        out_shape=jax.ShapeDtypeStruct((BATCH, SEQ, D_MODEL), F32),
        scratch_shapes=_FFN_SCRATCH,
        compiler_params=_params("arbitrary", "arbitrary"),
        name="attn_outproj_ffn_postnorm",
    )(attn, h_slabs, *[w for w, _ in outproj + ffn])


def _qkv_kernel(h_ref, w_ref, o_ref, lhs_ref, by4_ref):
    g = pl.program_id(1)
    which = pl.program_id(2)
    assert DILATIONS == (1, 4, 16)

    @pl.when(jnp.logical_and(g == 0, which == 0))
    def _():
        for k in range(N_SLABS):
            lhs_ref[:, k * LANES:(k + 1) * LANES] = h_ref[0, k].astype(BF16)

    @pl.when(jnp.logical_and(g == 1, which == 0))
    def _():
        slots = SEQ // 4
        for r in range(4):
            for k in range(N_SLABS):
                rows = h_ref[0, k, pl.ds(r, slots, stride=4), :]
                by4_ref[k, r * slots:(r + 1) * slots, :] = rows
                lhs_ref[r * slots:(r + 1) * slots, k * LANES:(k + 1) * LANES] = rows.astype(BF16)

    @pl.when(jnp.logical_and(g == 2, which == 0))
    def _():
        slots = SEQ // 16
        for r in range(16):
            for k in range(N_SLABS):
                rows = by4_ref[k, pl.ds((r % 4) * (SEQ // 4) + r // 4, slots, stride=4), :]
                lhs_ref[r * slots:(r + 1) * slots, k * LANES:(k + 1) * LANES] = rows.astype(BF16)

    scale = jnp.where(which == 0, HEAD_DIM ** -0.5 * LOG2_E, 1.0).astype(F32)
    for c in range(SEQ // PROJ_RC):
        rows = slice(c * PROJ_RC, (c + 1) * PROJ_RC)
        res = jnp.dot(lhs_ref[rows, :], w_ref[...], preferred_element_type=F32)
        o_ref[0, 0, rows, :] = (res * scale).astype(BF16)


def _qkv_project(h_slabs, w_qkv):
    return pl.pallas_call(
        _qkv_kernel,
        grid=(BATCH, N_DIL, 3),
        in_specs=[
            pl.BlockSpec((1, N_SLABS, SEQ, LANES), lambda b, g, w: (b, 0, 0, 0)),
            pl.BlockSpec((D_MODEL, D_MODEL), lambda b, g, w: (0, w * N_DIL + g)),
        ],
        out_specs=pl.BlockSpec((1, 1, SEQ, D_MODEL), lambda b, g, w: (b, g * 3 + w, 0, 0)),
        out_shape=jax.ShapeDtypeStruct((BATCH, 3 * N_DIL, SEQ, D_MODEL), BF16),
        scratch_shapes=[
            pltpu.VMEM((SEQ, D_MODEL), BF16),
            pltpu.VMEM((N_SLABS, SEQ, LANES), F32),
        ],
        compiler_params=_params("arbitrary", "arbitrary", "arbitrary"),
        name="qkv_project_regroup",
    )(h_slabs, w_qkv)


def _attn_kernel(qkv_ref, bias_ref, ones_ref, o_ref, og_ref, lse_ref, s_ref, p_ref, top_ref):
    lane = lax.broadcasted_iota(jnp.int32, (BAND, LANES), 1)
    lower = lane < HEAD_DIM

    def run_group(g):
        dil = DILATIONS[g]
        n_blocks = SEQ // dil // BAND
        with_prev = n_blocks > 1
        n_keys = 2 * BAND if with_prev else BAND
        n_items = SEQ // BAND
        lower_k = lax.broadcasted_iota(jnp.int32, (n_keys, LANES), 1) < HEAD_DIM

        def geometry(idx):
            row0 = pl.multiple_of(idx * BAND, BAND)
            if with_prev:
                first = jnp.asarray((idx % n_blocks) == 0)
                prev0 = pl.multiple_of(jnp.where(first, row0, row0 - BAND), BAND)
                kind = first.astype(jnp.int32)
            else:
                prev0, kind = None, None
            start = idx // n_blocks + dil * BAND * (idx % n_blocks)
            return row0, prev0, kind, start

        def time_rows(start):
            return pl.ds(start, BAND, stride=dil) if dil > 1 else pl.ds(start, BAND)

        def keys_or_values(which, row0, prev0, cols):
            cur = qkv_ref[0, 3 * g + which, pl.ds(row0, BAND), cols]
            if not with_prev:
                return cur
            return jnp.concatenate([qkv_ref[0, 3 * g + which, pl.ds(prev0, BAND), cols], cur], axis=0)

        def scores(idx):
            row0, prev0, _, _ = geometry(idx)
            slot = idx % ATT_RING
            for slab in range(ATT_SLABS):
                cols = slice(slab * LANES, (slab + 1) * LANES)
                q2 = qkv_ref[0, 3 * g, pl.ds(row0, BAND), cols]
                zero = jnp.zeros_like(q2)
                qq = jnp.concatenate([jnp.where(lower, q2, zero), jnp.where(lower, zero, q2)], axis=0)
                kk = keys_or_values(1, row0, prev0, cols)
                s_ref[slot, slab, :, :n_keys] = lax.dot_general(
                    qq, kk, (((1,), (1,)), ((), ())), preferred_element_type=F32)

        def softmax(idx):
            _, _, kind, _ = geometry(idx)
            slot = idx % ATT_RING
            for slab in range(ATT_SLABS):
                top = []
                for hh in range(2):
                    rows = slice(hh * BAND, (hh + 1) * BAND)
                    if with_prev:
                        bias = bias_ref[g, kind, 2 * slab + hh]
                    else:
                        bias = bias_ref[g, 0, 2 * slab + hh, :, BAND:]
                    s = s_ref[slot, slab, rows, :n_keys] + bias
                    m = jnp.max(s, axis=-1, keepdims=True)
                    p_ref[slot, slab, :, hh * n_keys:(hh + 1) * n_keys] = jnp.exp2(s - m).astype(BF16)
                    top.append(jnp.broadcast_to(m, (BAND, LANES)))
                top_ref[slot, slab] = jnp.where(lower, top[0], top[1])

        def output(idx):
            row0, prev0, _, start = geometry(idx)
            slot = idx % ATT_RING
            for slab in range(ATT_SLABS):
                cols = slice(slab * LANES, (slab + 1) * LANES)
                vv = keys_or_values(2, row0, prev0, cols)
                zero = jnp.zeros_like(vv)
                rhs = jnp.concatenate([
                    jnp.concatenate([jnp.where(lower_k, vv, zero), ones_ref[0, :n_keys, :]], axis=1),
                    jnp.concatenate([jnp.where(lower_k, zero, vv), ones_ref[1, :n_keys, :]], axis=1),
                ], axis=0)
                res = jnp.dot(p_ref[slot, slab, :, :2 * n_keys], rhs, preferred_element_type=F32)
                den = res[:, LANES:]
                o0 = res[:, :LANES] * (1.0 / den)
                l_own = top_ref[slot, slab] + jnp.log2(den)
                if g > 0:
                    og_ref[g - 1, slab, time_rows(start), :] = o0
                    lse_ref[g - 1, slab, time_rows(start), :] = l_own
                else:
                    rows = pl.ds(row0, BAND)
                    l0, l1, l2 = l_own, lse_ref[0, slab, rows, :], lse_ref[1, slab, rows, :]
                    top = jnp.maximum(jnp.maximum(l0, l1), l2)
                    w0, w1, w2 = jnp.exp2(l0 - top), jnp.exp2(l1 - top), jnp.exp2(l2 - top)
                    merged = (w0 * o0 + w1 * og_ref[0, slab, rows, :] + w2 * og_ref[1, slab, rows, :]) / (w0 + w1 + w2)
                    o_ref[0, rows, cols] = merged.astype(o_ref.dtype)

        stages = ((output, 2 * ATT_LAG), (softmax, ATT_LAG), (scores, 0))

        def run_steps(t0, n_steps, static):
            for stage, lag in stages:
                for j in range(n_steps):
                    idx = t0 + j - lag
                    if static:
                        if 0 <= idx < n_items:
                            stage(jnp.int32(idx))
                    else:
                        stage(idx)

        for t in range(2 * ATT_LAG):
            run_steps(t, 1, True)

        def steady(u, _):
            run_steps(2 * ATT_LAG + u * ATT_STEPS_PER_BODY, ATT_STEPS_PER_BODY, False)
            return 0
        lax.fori_loop(0, (n_items - 2 * ATT_LAG) // ATT_STEPS_PER_BODY, steady, 0)

        for t in range(n_items, n_items + 2 * ATT_LAG):
            run_steps(t, 1, True)

    for g in (2, 1, 0):
        run_group(g)


def _attention(qkv, bias):
    head_of_lane = np.arange(LANES) // HEAD_DIM
    ones = np.stack([np.broadcast_to(head_of_lane == hh, (2 * BAND, LANES)) for hh in range(2)])
    ones = jnp.asarray(ones, BF16)
    return pl.pallas_call(
        _attn_kernel,
        grid=(BATCH, HEADS // HEADS_PER_STEP),
        in_specs=[
            pl.BlockSpec((1, 3 * N_DIL, SEQ, ATT_W), lambda b, c: (b, 0, 0, c)),
            pl.BlockSpec((N_DIL, 2, HEADS_PER_STEP, BAND, 2 * BAND), lambda b, c: (0, 0, c, 0, 0)),
            _const_spec(ones.shape),
        ],
        out_specs=pl.BlockSpec((1, SEQ, ATT_W), lambda b, c: (b, 0, c)),
        out_shape=jax.ShapeDtypeStruct((BATCH, SEQ, D_MODEL), BF16),
        scratch_shapes=[
            pltpu.VMEM((N_DIL - 1, ATT_SLABS, SEQ, LANES), F32),
            pltpu.VMEM((N_DIL - 1, ATT_SLABS, SEQ, LANES), F32),
            pltpu.VMEM((ATT_RING, ATT_SLABS, 2 * BAND, 2 * BAND), F32),
            pltpu.VMEM((ATT_RING, ATT_SLABS, BAND, 4 * BAND), BF16),
            pltpu.VMEM((ATT_RING, ATT_SLABS, BAND, LANES), F32),
        ],
        compiler_params=_params("arbitrary", "arbitrary"),
        name="dilated_attention_merge",
    )(qkv, bias, ones)


def kernel(x, s5_lam_re, s5_lam_im, s5_log_dt, s5_b_re, s5_b_im, s5_c_re, s5_c_im, s5_d, s5_w_glu,
           s5_b_glu, s5_w_out, attn_w_kv, attn_w_q, attn_w_out, rel_bias, ffn_w_up, ffn_conv_w,
           ffn_conv_b, ffn_w_down, ln_gain, ln_bias):
    rows = lambda v: v.astype(F32).reshape(-1, 1, v.shape[-1])
    gains, biases = rows(ln_gain), rows(ln_bias)
    w_up, w_down = ffn_w_up.astype(BF16), ffn_w_down.astype(BF16)
    conv_w, conv_b = ffn_conv_w.astype(F32), rows(ffn_conv_b)
    ffn = lambda i: [(w_up, i), (conv_w, i), (conv_b, i), (w_down, i), (gains, 2 * i + 1), (biases, 2 * i + 1)]

    a, bb, cc = _s5_discretise(s5_lam_re[0], s5_lam_im[0], s5_log_dt[0], s5_b_re[0], s5_b_im[0],
                               s5_c_re[0], s5_c_im[0])
    d_skip = s5_d.astype(F32).reshape(-1, 1, D_MODEL)
    h = _s5_layer(x, a, bb, cc, [(d_skip, 0), (s5_w_glu.astype(BF16), 0), (rows(s5_b_glu), 0),
                                 (s5_w_out.astype(BF16), 0), (gains, 0), (biases, 0)])
    h_slabs = _ffn_layer(h, ffn(0))

    w_qkv = jnp.concatenate([attn_w_q[0], attn_w_kv], axis=1).astype(BF16)
    qkv = _qkv_project(h_slabs, w_qkv)

    buckets, in_band = _bucket_tables()
    bias = _bias_tiles(rel_bias.astype(F32), jnp.asarray(buckets), jnp.asarray(in_band))
    attn = _attention(qkv, bias)
    outproj = [(attn_w_out.astype(BF16), 0), (gains, 2), (biases, 2)]
    return _outproj_ffn_layer(attn, h_slabs, outproj, ffn(1))
```
